```python
import math
import jax
import jax.numpy as jnp
from jax import lax
import numpy as np

D_MODEL = 4096
BATCH = 2
SEQ = 8192
DEPTH = 2

CHUNK = 64
Q_BLOCK = 128
MIX_WIDTH = D_MODEL
ATTN_WIDTH = MIX_WIDTH // 2
SSM_WIDTH = MIX_WIDTH - ATTN_WIDTH
ATTN_HEAD_DIM = 128
N_ATTN_HEADS = ATTN_WIDTH // (2 * ATTN_HEAD_DIM)
SSM_GROUP = 16
N_SSM_GROUPS = SSM_WIDTH // SSM_GROUP
SSM_STATE = 64
IN_WIDTH = 3 * ATTN_WIDTH + SSM_WIDTH
D_FF = 256 * ((8 * D_MODEL // 3 + 255) // 256)
N_EXPERTS = 8
TOP_K = 2
D_FF_EXPERT = 7 * D_MODEL // 8
N_DENSE = (DEPTH + 1) // 2
N_MOE = DEPTH // 2
DEEPNORM_ALPHA = (2.0 * DEPTH) ** 0.25
DEEPNORM_BETA = (8.0 * DEPTH) ** -0.25
LN_EPS = 1e-5
RMS_EPS = 1e-5
DT_MIN = 1e-3
DT_MAX = 1e-1
MASK_VALUE = -1e30

kernel_name = 'hybrid_diffattn_s5_moe_deepnorm'


def _layer_norm(x, g, b):
    xf = x.astype(jnp.float32)
    mu = jnp.mean(xf, axis=-1, keepdims=True)
    var = jnp.mean(jnp.square(xf - mu), axis=-1, keepdims=True)
    return ((xf - mu) * lax.rsqrt(var + LN_EPS) * g + b).astype(x.dtype)


def _head_rms_norm(o, w):
    of = o.astype(jnp.float32)
    return of * lax.rsqrt(jnp.mean(of * of, axis=-1, keepdims=True) + RMS_EPS) * w


def _diff_attention(q, k, v, lam):
    bsz, seq = q.shape[0], q.shape[1]
    n_qb = seq // Q_BLOCK
    scale = ATTN_HEAD_DIM ** -0.5
    k_chunk = jnp.arange(seq) // CHUNK
    q_blocks = jnp.moveaxis(q.reshape(bsz, n_qb, Q_BLOCK, N_ATTN_HEADS, 2, ATTN_HEAD_DIM), 1, 0)

    def block(args):
        qb, i = args
        s = jnp.einsum('bqhmd,bkhmd->bhmqk', qb, k).astype(jnp.float32) * scale
        q_chunk = (i * Q_BLOCK + jnp.arange(Q_BLOCK)) // CHUNK
        allowed = k_chunk[None, :] <= q_chunk[:, None]
        p = jax.nn.softmax(jnp.where(allowed, s, MASK_VALUE), axis=-1)
        w = p[:, :, 0] - lam * p[:, :, 1]
        return jnp.einsum('bhqk,bkhe->bqhe', w.astype(v.dtype), v)

    out = lax.map(block, (q_blocks, jnp.arange(n_qb)))
    return jnp.moveaxis(out, 0, 1).reshape(bsz, seq, N_ATTN_HEADS, 2 * ATTN_HEAD_DIM)


def _s5(u, a_re, a_im, log_dt, b_re, b_im, c_re, c_im, d_skip):
    f32 = jnp.float32
    bsz, seq = u.shape[0], u.shape[1]
    uf = u.astype(f32).reshape(bsz, seq // CHUNK, CHUNK, N_SSM_GROUPS, SSM_GROUP)
    uf = jnp.moveaxis(uf, 1, 0)
    a_re = a_re.astype(f32)
    a_im = a_im.astype(f32)
    dt = jnp.exp(log_dt.astype(f32))[:, None]
    mag = jnp.exp(a_re * dt)
    ang = a_im * dt
    lb_re = mag * jnp.cos(ang)
    lb_im = mag * jnp.sin(ang)
    den = a_re * a_re + a_im * a_im
    f_re = ((lb_re - 1.0) * a_re + lb_im * a_im) / den
    f_im = (lb_im * a_re - (lb_re - 1.0) * a_im) / den
    b_re = b_re.astype(f32)
    b_im = b_im.astype(f32)
    bb_re = f_re[..., None] * b_re - f_im[..., None] * b_im
    bb_im = f_re[..., None] * b_im + f_im[..., None] * b_re
    c_re = c_re.astype(f32)
    c_im = c_im.astype(f32)
    d = d_skip.astype(f32).reshape(N_SSM_GROUPS, SSM_GROUP)

    def combine(e1, e2):
        a1r, a1i, b1r, b1i = e1
        a2r, a2i, b2r, b2i = e2
        return (a2r * a1r - a2i * a1i, a2r * a1i + a2i * a1r,
                a2r * b1r - a2i * b1i + b2r, a2r * b1i + a2i * b1r + b2i)

    def step(carry, uc):
        h_re, h_im = carry
        bu_re = jnp.einsum('btgp,gnp->btgn', uc, bb_re)
        bu_im = jnp.einsum('btgp,gnp->btgn', uc, bb_im)
        ar = jnp.broadcast_to(lb_re, bu_re.shape)
        ai = jnp.broadcast_to(lb_im, bu_re.shape)
        pa_re, pa_im, s_re, s_im = lax.associative_scan(combine, (ar, ai, bu_re, bu_im), axis=1)
        x_re = s_re + pa_re * h_re[:, None] - pa_im * h_im[:, None]
        x_im = s_im + pa_re * h_im[:, None] + pa_im * h_re[:, None]
        y = (jnp.einsum('btgn,gpn->btgp', x_re, c_re)
             - jnp.einsum('btgn,gpn->btgp', x_im, c_im) + d * uc)
        return (x_re[:, -1], x_im[:, -1]), y

    h0 = jnp.zeros((bsz, N_SSM_GROUPS, SSM_STATE), f32)
    _, ys = lax.scan(step, (h0, h0), uf)
    return jnp.moveaxis(ys, 0, 1).reshape(bsz, seq, SSM_WIDTH).astype(u.dtype)


def _swiglu(x, w_gate, w_up, w_down):
    return (jax.nn.silu(x @ w_gate) * (x @ w_up)) @ w_down


def _moe_swiglu(x, w_router, w_gate, w_up, w_down):
    logits = (x @ w_router).astype(jnp.float32)
    top_val, top_idx = lax.top_k(logits, TOP_K)
    top_w = jax.nn.softmax(top_val, axis=-1)
    gates = jnp.sum(jax.nn.one_hot(top_idx, N_EXPERTS, dtype=jnp.float32) * top_w[..., None], axis=-2)
    y = jnp.zeros(x.shape, jnp.float32)
    for e in range(N_EXPERTS):
        y = y + gates[..., e:e + 1] * _swiglu(x, w_gate[e], w_up[e], w_down[e]).astype(jnp.float32)
    return y.astype(x.dtype)


def setup_inputs(seed: int = 0) -> dict:
    key = jax.random.key(seed)
    ks = jax.random.split(key, 28)
    f32 = jnp.float32

    def nrm(k, shape, scale):
        return jax.random.normal(k, shape, f32) * scale

    n_idx = jnp.arange(SSM_STATE, dtype=f32)
    G, N, P = N_SSM_GROUPS, SSM_STATE, SSM_GROUP
    return {
        'x': nrm(ks[0], (BATCH, SEQ, D_MODEL), 1.0),
        'w_in': nrm(ks[1], (DEPTH, D_MODEL, IN_WIDTH), D_MODEL ** -0.5),
        'w_out': nrm(ks[2], (DEPTH, MIX_WIDTH, D_MODEL), DEEPNORM_BETA * MIX_WIDTH ** -0.5),
        'attn_lambda_q1': nrm(ks[3], (DEPTH, ATTN_HEAD_DIM), 0.1),
        'attn_lambda_k1': nrm(ks[4], (DEPTH, ATTN_HEAD_DIM), 0.1),
        'attn_lambda_q2': nrm(ks[5], (DEPTH, ATTN_HEAD_DIM), 0.1),
        'attn_lambda_k2': nrm(ks[6], (DEPTH, ATTN_HEAD_DIM), 0.1),
        'attn_subln_w': 1.0 + nrm(ks[7], (DEPTH, 2 * ATTN_HEAD_DIM), 0.02),
        'ssm_a_re': -0.5 + nrm(ks[8], (DEPTH, G, N), 0.01),
        'ssm_a_im': math.pi * n_idx + nrm(ks[9], (DEPTH, G, N), 0.01),
        'ssm_log_dt': jax.random.uniform(ks[10], (DEPTH, G), f32, math.log(DT_MIN), math.log(DT_MAX)),
        'ssm_b_re': nrm(ks[11], (DEPTH, G, N, P), (2 * P) ** -0.5),
        'ssm_b_im': nrm(ks[12], (DEPTH, G, N, P), (2 * P) ** -0.5),
        'ssm_c_re': nrm(ks[13], (DEPTH, G, P, N), N ** -0.5),
        'ssm_c_im': nrm(ks[14], (DEPTH, G, P, N), N ** -0.5),
        'ssm_d': nrm(ks[15], (DEPTH, SSM_WIDTH), 1.0),
        'ssm_w_glu': nrm(ks[16], (DEPTH, SSM_WIDTH, SSM_WIDTH), SSM_WIDTH ** -0.5),
        'ln1_g': 1.0 + nrm(ks[17], (DEPTH, D_MODEL), 0.02),
        'ln1_b': nrm(ks[18], (DEPTH, D_MODEL), 0.02),
        'ln2_g': 1.0 + nrm(ks[19], (DEPTH, D_MODEL), 0.02),
        'ln2_b': nrm(ks[20], (DEPTH, D_MODEL), 0.02),
        'ffn_w_gate': nrm(ks[21], (N_DENSE, D_MODEL, D_FF), D_MODEL ** -0.5),
        'ffn_w_up': nrm(ks[22], (N_DENSE, D_MODEL, D_FF), D_MODEL ** -0.5),
        'ffn_w_down': nrm(ks[23], (N_DENSE, D_FF, D_MODEL), DEEPNORM_BETA * D_FF ** -0.5),
        'moe_w_router': nrm(ks[24], (N_MOE, D_MODEL, N_EXPERTS), D_MODEL ** -0.5),
        'moe_w_gate': nrm(ks[25], (N_MOE, N_EXPERTS, D_MODEL, D_FF_EXPERT), D_MODEL ** -0.5),
        'moe_w_up': nrm(ks[26], (N_MOE, N_EXPERTS, D_MODEL, D_FF_EXPERT), D_MODEL ** -0.5),
        'moe_w_down': nrm(ks[27], (N_MOE, N_EXPERTS, D_FF_EXPERT, D_MODEL), DEEPNORM_BETA * D_FF_EXPERT ** -0.5),
    }


def reference(x, w_in, w_out, attn_lambda_q1, attn_lambda_k1, attn_lambda_q2, attn_lambda_k2,
              attn_subln_w, ssm_a_re, ssm_a_im, ssm_log_dt, ssm_b_re, ssm_b_im, ssm_c_re, ssm_c_im,
              ssm_d, ssm_w_glu, ln1_g, ln1_b, ln2_g, ln2_b, ffn_w_gate, ffn_w_up, ffn_w_down,
              moe_w_router, moe_w_gate, moe_w_up, moe_w_down):
    bsz, seq = x.shape[0], x.shape[1]
    f32 = jnp.float32
    for l in range(DEPTH):
        lam_init = 0.8 - 0.6 * math.exp(-0.3 * l)
        h = x @ w_in[l]
        q, k, v, u = jnp.split(h, [ATTN_WIDTH, 2 * ATTN_WIDTH, 3 * ATTN_WIDTH], axis=-1)
        q = q.reshape(bsz, seq, N_ATTN_HEADS, 2, ATTN_HEAD_DIM)
        k = k.reshape(bsz, seq, N_ATTN_HEADS, 2, ATTN_HEAD_DIM)
        v = v.reshape(bsz, seq, N_ATTN_HEADS, 2 * ATTN_HEAD_DIM)
        lam = (jnp.exp(jnp.sum(attn_lambda_q1[l].astype(f32) * attn_lambda_k1[l].astype(f32)))
               - jnp.exp(jnp.sum(attn_lambda_q2[l].astype(f32) * attn_lambda_k2[l].astype(f32)))
               + lam_init)
        o = _diff_attention(q, k, v, lam)
        o = _head_rms_norm(o, attn_subln_w[l]) * (1.0 - lam_init)
        o = o.reshape(bsz, seq, ATTN_WIDTH).astype(x.dtype)
        y = _s5(u, ssm_a_re[l], ssm_a_im[l], ssm_log_dt[l], ssm_b_re[l], ssm_b_im[l],
                ssm_c_re[l], ssm_c_im[l], ssm_d[l])
        g = jax.nn.gelu(y, approximate=False)
        y = g * jax.nn.sigmoid(g @ ssm_w_glu[l])
        mix = jnp.concatenate([o, y], axis=-1) @ w_out[l]
        x = _layer_norm(DEEPNORM_ALPHA * x + mix, ln1_g[l], ln1_b[l])
        if l % 2 == 0:
            f = _swiglu(x, ffn_w_gate[l // 2], ffn_w_up[l // 2], ffn_w_down[l // 2])
        else:
            f = _moe_swiglu(x, moe_w_router[l // 2], moe_w_gate[l // 2], moe_w_up[l // 2], moe_w_down[l // 2])
        x = _layer_norm(DEEPNORM_ALPHA * x + f, ln2_g[l], ln2_b[l])
    return x
```

```python
import functools
import math

import jax
import jax.numpy as jnp
from jax import lax
from jax.experimental import pallas as pl
from jax.experimental.pallas import tpu as pltpu

F32 = jnp.float32
BF16 = jnp.bfloat16

V7X_LANES = 128
V7X_SUBLANES = 8
V7X_VMEM_LIMIT_BYTES = 56 * 1024 * 1024

ATTN_CHUNK = 64
TOP_K = 2
LN_EPS = 1e-5
RMS_EPS = 1e-5
MASK_VALUE = -1e30
S5_CHUNK = 64


def _pick(dim, target, align):
    if dim <= target:
        return dim
    t = (target // align) * align
    while t >= align:
        if dim % t == 0:
            return t
        t -= align
    raise ValueError(f"no tile for dim={dim} target={target} align={align}")


def _params(semantics):
    return pltpu.CompilerParams(dimension_semantics=semantics,
                                vmem_limit_bytes=V7X_VMEM_LIMIT_BYTES)


def _mm_kernel(*refs, n_rhs, n_row, n_col, epilogue):
    lhs_ref = refs[0]
    rhs_refs = refs[1:1 + n_rhs]
    row_refs = refs[1 + n_rhs:1 + n_rhs + n_row]
    col_refs = refs[1 + n_rhs + n_row:1 + n_rhs + n_row + n_col]
    o_ref = refs[-1]
    a = lhs_ref[...]
    accs = [jnp.dot(a, r[...], preferred_element_type=F32) for r in rhs_refs]
    out = epilogue(accs, [r[...] for r in row_refs], [c[...] for c in col_refs])
    o_ref[...] = out.astype(o_ref.dtype)


def _matmul(lhs, rhss, epilogue, out_dtype, *, tm, tn, row_extras=(), col_extras=(), name):
    m, k = lhs.shape
    n = rhss[0].shape[1]
    tm = _pick(m, tm, V7X_SUBLANES * 2)
    tn = _pick(n, tn, V7X_LANES)
    in_specs = [pl.BlockSpec((tm, k), lambda i, j: (i, 0))]
    in_specs += [pl.BlockSpec((k, tn), lambda i, j: (0, j)) for _ in rhss]
    in_specs += [pl.BlockSpec((tm, tn), lambda i, j: (i, j)) for _ in row_extras]
    in_specs += [pl.BlockSpec((1, tn), lambda i, j: (0, j)) for _ in col_extras]
    body = functools.partial(_mm_kernel, n_rhs=len(rhss), n_row=len(row_extras),
                             n_col=len(col_extras), epilogue=epilogue)
    return pl.pallas_call(
        body,
        grid=(m // tm, n // tn),
        in_specs=in_specs,
        out_specs=pl.BlockSpec((tm, tn), lambda i, j: (i, j)),
        out_shape=jax.ShapeDtypeStruct((m, n), out_dtype),
        compiler_params=_params(("parallel", "arbitrary")),
        name=name,
    )(lhs, *rhss, *row_extras, *col_extras)


def _epi_scale_cols(accs, rows, cols):
    return accs[0] * cols[0]


def _epi_swiglu(accs, rows, cols):
    g = accs[0]
    return g * jax.nn.sigmoid(g) * accs[1]


def _epi_glu(accs, rows, cols):
    g = rows[0].astype(F32)
    return g * jax.nn.sigmoid(accs[0])


def _layer_norm_rows(r, g, b):
    mu = jnp.mean(r, axis=-1, keepdims=True)
    c = r - mu
    var = jnp.mean(c * c, axis=-1, keepdims=True)
    return c * lax.rsqrt(var + LN_EPS) * g + b


def _mm_ln_kernel(lhs_ref, rhs_ref, x_ref, g_ref, b_ref, of_ref, ob_ref, *, alpha):
    kk = pl.program_id(1)

    @pl.when(kk == 0)
    def _():
        of_ref[...] = jnp.zeros_like(of_ref)

    of_ref[...] += jnp.dot(lhs_ref[...], rhs_ref[...], preferred_element_type=F32)

    @pl.when(kk == pl.num_programs(1) - 1)
    def _():
        y = _layer_norm_rows(alpha * x_ref[...] + of_ref[...], g_ref[...], b_ref[...])
        of_ref[...] = y
        ob_ref[...] = y.astype(BF16)


def _matmul_ln(lhs, rhs, x, g, b, *, alpha, tm, tk, name):
    m, k = lhs.shape
    n = rhs.shape[1]
    tm = _pick(m, tm, V7X_SUBLANES * 2)
    tk = _pick(k, tk, V7X_LANES)
    return pl.pallas_call(
        functools.partial(_mm_ln_kernel, alpha=alpha),
        grid=(m // tm, k // tk),
        in_specs=[
            pl.BlockSpec((tm, tk), lambda i, kk: (i, kk)),
            pl.BlockSpec((tk, n), lambda i, kk: (kk, 0)),
            pl.BlockSpec((tm, n), lambda i, kk: (i, 0), pipeline_mode=pl.Buffered(1)),
            pl.BlockSpec((1, n), lambda i, kk: (0, 0)),
            pl.BlockSpec((1, n), lambda i, kk: (0, 0)),
        ],
        out_specs=[pl.BlockSpec((tm, n), lambda i, kk: (i, 0)),
                   pl.BlockSpec((tm, n), lambda i, kk: (i, 0))],
        out_shape=[jax.ShapeDtypeStruct((m, n), F32), jax.ShapeDtypeStruct((m, n), BF16)],
        compiler_params=_params(("parallel", "arbitrary")),
        name=name,
    )(lhs, rhs, x, g.reshape(1, n), b.reshape(1, n))


def _attn_kernel(lam_ref, w_ref, q_ref, k_ref, v_ref, o_ref, m_sc, l_sc, acc_sc, *,
                 tq, d, lam_init):
    qi = pl.program_id(2)
    m_sc[...] = jnp.full(m_sc.shape, MASK_VALUE, F32)
    l_sc[...] = jnp.zeros_like(l_sc)
    acc_sc[...] = jnp.zeros_like(acc_sc)

    def step(j, masked):
        start = pl.multiple_of(j * tq, tq)
        kb = k_ref[pl.ds(start, tq), :]
        vb = v_ref[pl.ds(start, tq), :]
        if masked:
            row = lax.broadcasted_iota(jnp.int32, (tq, tq), 0) // ATTN_CHUNK
            col = lax.broadcasted_iota(jnp.int32, (tq, tq), 1) // ATTN_CHUNK
            allowed = col <= row
        for mp in range(2):
            qm = q_ref[:, mp * d:(mp + 1) * d]
            km = kb[:, mp * d:(mp + 1) * d]
            s = lax.dot_general(qm, km, (((1,), (1,)), ((), ())), preferred_element_type=F32)
            if masked:
                s = jnp.where(allowed, s, MASK_VALUE)
            m_old = m_sc[mp]
            m_new = jnp.maximum(m_old, jnp.max(s, axis=-1, keepdims=True))
            alpha = jnp.exp(m_old - m_new)
            p = jnp.exp(s - m_new)
            l_sc[mp] = alpha * l_sc[mp] + jnp.sum(p, axis=-1, keepdims=True)
            acc_sc[mp] = alpha * acc_sc[mp] + jnp.dot(p.astype(BF16), vb,
                                                      preferred_element_type=F32)
            m_sc[mp] = m_new

    def body(j, carry):
        step(j, False)
        return carry

    lax.fori_loop(0, qi, body, 0)
    step(qi, True)

    lam_p = lam_ref[...]
    lam = (jnp.exp(jnp.sum(lam_p[0:1] * lam_p[1:2], axis=-1, keepdims=True))
           - jnp.exp(jnp.sum(lam_p[2:3] * lam_p[3:4], axis=-1, keepdims=True)) + lam_init)
    o = acc_sc[0] / l_sc[0] - lam * (acc_sc[1] / l_sc[1])
    o = o * lax.rsqrt(jnp.mean(o * o, axis=-1, keepdims=True) + RMS_EPS)
    o_ref[...] = (o * w_ref[...] * (1.0 - lam_init)).astype(o_ref.dtype)


def _diff_attention(h, lam_params, subln_w, *, n_heads, d, lam_init, tq):
    bsz, seq, _ = h.shape
    hw = 2 * d
    tq = _pick(seq, tq, ATTN_CHUNK)
    return pl.pallas_call(
        functools.partial(_attn_kernel, tq=tq, d=d, lam_init=lam_init),
        grid=(bsz, n_heads, seq // tq),
        in_specs=[
            pl.BlockSpec((4, d), lambda b, hd, i: (0, 0)),
            pl.BlockSpec((1, hw), lambda b, hd, i: (0, 0)),
            pl.BlockSpec((None, tq, hw), lambda b, hd, i: (b, i, hd)),
            pl.BlockSpec((None, seq, hw), lambda b, hd, i: (b, 0, n_heads + hd)),
            pl.BlockSpec((None, seq, hw), lambda b, hd, i: (b, 0, 2 * n_heads + hd)),
        ],
        out_specs=pl.BlockSpec((None, tq, hw), lambda b, hd, i: (b, i, hd)),
        out_shape=jax.ShapeDtypeStruct((bsz, seq, n_heads * hw), BF16),
        scratch_shapes=[pltpu.VMEM((2, tq, 1), F32), pltpu.VMEM((2, tq, 1), F32),
                        pltpu.VMEM((2, tq, hw), F32)],
        compiler_params=_params(("parallel", "parallel", "arbitrary")),
        name="diff_attention",
    )(lam_params, subln_w.reshape(1, hw), h, h, h)


def _s5_kernel(u_ref, acol_ref, arow_ref, bt_ref, bx_ref, ct_ref, d_ref, o_ref, m_ref, *,
               tc, p, n, n_chunks):
    tcp = tc * p
    u = u_ref[...]
    rows = u.shape[0]

    ar = acol_ref[:, 0:1]
    ai = acol_ref[:, 1:2]
    dt = acol_ref[:, 2:3]
    tau = (lax.broadcasted_iota(jnp.int32, (1, tcp), 1) // p).astype(F32)

    def power_table(t):
        mag = jnp.exp((ar * dt) * t)
        ang = (ai * dt) * t
        return mag * jnp.cos(ang), mag * jnp.sin(ang)

    mag1 = jnp.exp(ar * dt)
    lr = mag1 * jnp.cos(ai * dt)
    li = mag1 * jnp.sin(ai * dt)
    den = ar * ar + ai * ai
    f_re = ((lr - 1.0) * ar + li * ai) / den
    f_im = (li * ar - (lr - 1.0) * ai) / den
    bt_re = bt_ref[0]
    bt_im = bt_ref[1]
    bb_re = f_re * bt_re - f_im * bt_im
    bb_im = f_re * bt_im + f_im * bt_re

    pr_re, pr_im = power_table((tc - 1.0) - tau)
    z_re = pr_re * bb_re - pr_im * bb_im
    z_im = pr_re * bb_im + pr_im * bb_re
    wt = jnp.concatenate([z_re, z_im], axis=0).astype(BF16)

    pf_re, pf_im = power_table(tau)
    ct_re = ct_ref[0]
    ct_im = ct_ref[1]
    v_re = pf_re * ct_re - pf_im * ct_im
    v_im = pf_re * ct_im + pf_im * ct_re
    vmat = jnp.concatenate([v_re, -v_im], axis=0)

    ar2 = arow_ref[0:1, :]
    ai2 = arow_ref[1:2, :]
    dt2 = arow_ref[2:3, :]
    half = lax.broadcasted_iota(jnp.int32, (1, 2 * n), 1) < n
    sgn = jnp.where(half, -1.0, 1.0).astype(F32)
    mag1r = jnp.exp(ar2 * dt2)
    lr2 = mag1r * jnp.cos(ai2 * dt2)
    li2 = mag1r * jnp.sin(ai2 * dt2)
    den2 = ar2 * ar2 + ai2 * ai2
    fr2 = ((lr2 - 1.0) * ar2 + li2 * ai2) / den2
    fi2 = (li2 * ar2 - (lr2 - 1.0) * ai2) / den2
    bbt = fr2 * bx_ref[0] + (sgn * fi2) * bx_ref[1]
    kq = jnp.dot(bbt, vmat, preferred_element_type=F32, precision=lax.Precision.HIGHEST)

    lane = lax.broadcasted_iota(jnp.int32, (p, tcp), 1)
    m_ref[0:p, :] = kq.astype(BF16)
    for s in range(1, tc):
        blk = jnp.where(lane >= s * p, pltpu.roll(kq, s * p, axis=1), 0.0)
        m_ref[s * p:(s + 1) * p, :] = blk.astype(BF16)

    def cmul(x, r2, i2s):
        return x * r2 + pltpu.roll(x, n, axis=1) * i2s

    st = lax.dot_general(u, wt, (((1,), (1,)), ((), ())), preferred_element_type=F32)
    cidx = lax.broadcasted_iota(jnp.int32, (rows, 2 * n), 0) % n_chunks
    mag_c = jnp.exp(ar2 * dt2 * tc)
    ang_c = ai2 * dt2 * tc
    r2 = mag_c * jnp.cos(ang_c)
    i2 = sgn * mag_c * jnp.sin(ang_c)
    k = 1
    while k < n_chunks:
        sh = jnp.where(cidx >= k, pltpu.roll(st, k, axis=0), 0.0)
        st = st + cmul(sh, r2, i2)
        r2, i2 = r2 * r2 - i2 * i2, 2.0 * r2 * i2
        k *= 2
    hin = jnp.where(cidx >= 1, pltpu.roll(st, 1, axis=0), 0.0)
    hp = cmul(hin, lr2, sgn * li2)

    y = jnp.dot(u, m_ref[...], preferred_element_type=F32)
    y = y + jnp.dot(hp.astype(BF16), vmat.astype(BF16), preferred_element_type=F32)
    y = y + u.astype(F32) * d_ref[...]
    g = 0.5 * y * (1.0 + lax.erf(y * (2.0 ** -0.5)))
    o_ref[...] = g.astype(o_ref.dtype)


def _s5_gelu(u, a_re, a_im, log_dt, b_re, b_im, c_re, c_im, d_skip):
    bsz, seq, width = u.shape
    n_groups, n = a_re.shape
    p = b_re.shape[-1]
    tc = S5_CHUNK
    n_chunks = seq // tc
    rows = bsz * n_chunks
    tcp = tc * p
    uf = u.reshape(bsz, n_chunks, tc, n_groups, p)
    uf = jnp.transpose(uf, (3, 0, 1, 2, 4)).reshape(n_groups, rows, tcp)
    dt = jnp.exp(log_dt.astype(F32))
    dtb = jnp.broadcast_to(dt[:, None], (n_groups, n))
    acol = jnp.stack([a_re.astype(F32), a_im.astype(F32), dtb], axis=-1)
    arow = jnp.stack([jnp.tile(a_re.astype(F32), (1, 2)), jnp.tile(a_im.astype(F32), (1, 2)),
                      jnp.tile(dtb, (1, 2))], axis=1)
    bt = jnp.stack([jnp.tile(b_re.astype(F32), (1, 1, tc)),
                    jnp.tile(b_im.astype(F32), (1, 1, tc))], axis=1)
    btr = jnp.swapaxes(b_re.astype(F32), 1, 2)
    bti = jnp.swapaxes(b_im.astype(F32), 1, 2)
    bx = jnp.stack([jnp.concatenate([btr, bti], axis=-1),
                    jnp.concatenate([bti, btr], axis=-1)], axis=1)
    ctr = jnp.swapaxes(c_re.astype(F32), 1, 2)
    cti = jnp.swapaxes(c_im.astype(F32), 1, 2)
    ct = jnp.stack([jnp.tile(ctr, (1, 1, tc)), jnp.tile(cti, (1, 1, tc))], axis=1)
    dtile = jnp.tile(d_skip.astype(F32).reshape(n_groups, 1, p), (1, 1, tc))

    out = pl.pallas_call(
        functools.partial(_s5_kernel, tc=tc, p=p, n=n, n_chunks=n_chunks),
        grid=(n_groups,),
        in_specs=[
            pl.BlockSpec((None, rows, tcp), lambda g: (g, 0, 0)),
            pl.BlockSpec((None, n, 3), lambda g: (g, 0, 0)),
            pl.BlockSpec((None, 3, 2 * n), lambda g: (g, 0, 0)),
            pl.BlockSpec((None, 2, n, tcp), lambda g: (g, 0, 0, 0)),
            pl.BlockSpec((None, 2, p, 2 * n), lambda g: (g, 0, 0, 0)),
            pl.BlockSpec((None, 2, n, tcp), lambda g: (g, 0, 0, 0)),
            pl.BlockSpec((None, 1, tcp), lambda g: (g, 0, 0)),
        ],
        out_specs=pl.BlockSpec((None, rows, tcp), lambda g: (g, 0, 0)),
        out_shape=jax.ShapeDtypeStruct((n_groups, rows, tcp), BF16),
        scratch_shapes=[pltpu.VMEM((tcp, tcp), BF16)],
        compiler_params=_params(("parallel",)),
        name="s5_gelu",
    )(uf, acol, arow, bt, bx, ct, dtile)
    out = out.reshape(n_groups, bsz, n_chunks, tc, p)
    return jnp.transpose(out, (1, 2, 3, 0, 4)).reshape(bsz, seq, width)


def _router_kernel(x_ref, w_ref, idx_ref, gate_ref, *, n_experts):
    logits = jnp.dot(x_ref[...], w_ref[...], preferred_element_type=F32,
                     precision=lax.Precision.HIGHEST)
    col = lax.broadcasted_iota(jnp.int32, logits.shape, 1)
    big = jnp.int32(logits.shape[1])
    lg = jnp.where(col < n_experts, logits, -jnp.inf)
    m1 = jnp.max(lg, axis=-1, keepdims=True)
    i1 = jnp.min(jnp.where(lg == m1, col, big), axis=-1, keepdims=True)
    lg2 = jnp.where(col == i1, -jnp.inf, lg)
    m2 = jnp.max(lg2, axis=-1, keepdims=True)
    i2 = jnp.min(jnp.where(lg2 == m2, col, big), axis=-1, keepdims=True)
    e = jnp.exp(m2 - m1)
    w1 = 1.0 / (1.0 + e)
    w2 = e / (1.0 + e)
    idx_ref[...] = jnp.where(col == 0, i1, jnp.where(col == 1, i2, 0))
    gate_ref[...] = jnp.where(col == 0, w1, jnp.where(col == 1, w2, 0.0))


def _router(x, w_router, *, tm):
    t, dm = x.shape
    n_experts = w_router.shape[1]
    wp = jnp.zeros((dm, V7X_LANES), F32).at[:, :n_experts].set(w_router.astype(F32))
    tm = _pick(t, tm, V7X_SUBLANES)
    idx, gate = pl.pallas_call(
        functools.partial(_router_kernel, n_experts=n_experts),
        grid=(t // tm,),
        in_specs=[pl.BlockSpec((tm, dm), lambda i: (i, 0)),
                  pl.BlockSpec((dm, V7X_LANES), lambda i: (0, 0))],
        out_specs=[pl.BlockSpec((tm, V7X_LANES), lambda i: (i, 0)),
                   pl.BlockSpec((tm, V7X_LANES), lambda i: (i, 0))],
        out_shape=[jax.ShapeDtypeStruct((t, V7X_LANES), jnp.int32),
                   jax.ShapeDtypeStruct((t, V7X_LANES), F32)],
        compiler_params=_params(("parallel",)),
        name="moe_router",
    )(x, wp)
    return idx[:, :TOP_K], gate[:, :TOP_K]


def _gather_kernel(tok_ref, x_hbm, o_ref, buf, sem, *, tg):
    base = pl.program_id(0) * tg

    def row_copy(r):
        return pltpu.make_async_copy(x_hbm.at[pl.ds(tok_ref[base + r], 1), :],
                                     buf.at[pl.ds(r, 1), :], sem)

    def issue(r, carry):
        row_copy(r).start()
        return carry

    def drain(r, carry):
        row_copy(r).wait()
        return carry

    lax.fori_loop(0, tg, issue, 0)
    lax.fori_loop(0, tg, drain, 0)
    o_ref[...] = buf[...].astype(o_ref.dtype)


def _gather_rows(x, row_token, *, tg):
    n_rows = row_token.shape[0]
    dm = x.shape[1]
    tg = _pick(n_rows, tg, V7X_SUBLANES * 2)
    return pl.pallas_call(
        functools.partial(_gather_kernel, tg=tg),
        grid_spec=pltpu.PrefetchScalarGridSpec(
            num_scalar_prefetch=1,
            grid=(n_rows // tg,),
            in_specs=[pl.BlockSpec(memory_space=pl.ANY)],
            out_specs=pl.BlockSpec((tg, dm), lambda i, tok: (i, 0)),
            scratch_shapes=[pltpu.VMEM((tg, dm), x.dtype), pltpu.SemaphoreType.DMA(())],
        ),
        out_shape=jax.ShapeDtypeStruct((n_rows, dm), BF16),
        compiler_params=_params(("arbitrary",)),
        name="moe_gather",
    )(row_token, x)


def _grouped_kernel(te_ref, tv_ref, *refs, n_rhs, n_row, epilogue):
    lhs_ref = refs[0]
    rhs_refs = refs[1:1 + n_rhs]
    row_refs = refs[1 + n_rhs:1 + n_rhs + n_row]
    o_ref = refs[-1]

    valid = tv_ref[pl.program_id(0)] != 0

    @pl.when(valid)
    def _():
        a = lhs_ref[...]
        accs = [jnp.dot(a, r[...], preferred_element_type=F32) for r in rhs_refs]
        o_ref[...] = epilogue(accs, [r[...] for r in row_refs], []).astype(o_ref.dtype)

    @pl.when(jnp.logical_not(valid))
    def _():
        o_ref[...] = jnp.zeros_like(o_ref)


def _grouped_matmul(lhs, rhss, tile_expert, tile_valid, epilogue, out_dtype, *, tm, tn,
                    row_extras=(), name):
    r, k = lhs.shape
    n = rhss[0].shape[2]
    tn = _pick(n, tn, V7X_LANES)
    in_specs = [pl.BlockSpec((tm, k), lambda i, j, te, tv: (i, 0))]
    in_specs += [pl.BlockSpec((None, k, tn), lambda i, j, te, tv: (te[i], 0, j)) for _ in rhss]
    in_specs += [pl.BlockSpec((tm, 1), lambda i, j, te, tv: (i, 0)) for _ in row_extras]
    return pl.pallas_call(
        functools.partial(_grouped_kernel, n_rhs=len(rhss), n_row=len(row_extras),
                          epilogue=epilogue),
        grid_spec=pltpu.PrefetchScalarGridSpec(
            num_scalar_prefetch=2,
            grid=(r // tm, n // tn),
            in_specs=in_specs,
            out_specs=pl.BlockSpec((tm, tn), lambda i, j, te, tv: (i, j)),
        ),
        out_shape=jax.ShapeDtypeStruct((r, n), out_dtype),
        compiler_params=_params(("parallel", "arbitrary")),
        name=name,
    )(tile_expert, tile_valid, lhs, *rhss, *row_extras)


def _epi_row_scale(accs, rows, cols):
    return accs[0] * rows[0]


def _combine_ln_kernel(pos_ref, y_hbm, x_ref, g_ref, b_ref, of_ref, ob_ref, buf, sem, *,
                       tc, n_tokens, alpha):
    base = pl.program_id(0) * tc

    def row_copy(kk, r):
        return pltpu.make_async_copy(y_hbm.at[pl.ds(pos_ref[kk * n_tokens + base + r], 1), :],
                                     buf.at[kk, pl.ds(r, 1), :], sem)

    def issue(r, carry):
        for kk in range(TOP_K):
            row_copy(kk, r).start()
        return carry

    def drain(r, carry):
        for kk in range(TOP_K):
            row_copy(kk, r).wait()
        return carry

    lax.fori_loop(0, tc, issue, 0)
    lax.fori_loop(0, tc, drain, 0)
    f = buf[0]
    for kk in range(1, TOP_K):
        f = f + buf[kk]
    y = _layer_norm_rows(alpha * x_ref[...] + f, g_ref[...], b_ref[...])
    of_ref[...] = y
    ob_ref[...] = y.astype(BF16)


def _combine_ln(y_sorted, pos, x, g, b, *, alpha, tc):
    t, dm = x.shape
    tc = _pick(t, tc, V7X_SUBLANES * 2)
    return pl.pallas_call(
        functools.partial(_combine_ln_kernel, tc=tc, n_tokens=t, alpha=alpha),
        grid_spec=pltpu.PrefetchScalarGridSpec(
            num_scalar_prefetch=1,
            grid=(t // tc,),
            in_specs=[pl.BlockSpec(memory_space=pl.ANY),
                      pl.BlockSpec((tc, dm), lambda i, pos: (i, 0)),
                      pl.BlockSpec((1, dm), lambda i, pos: (0, 0)),
                      pl.BlockSpec((1, dm), lambda i, pos: (0, 0))],
            out_specs=[pl.BlockSpec((tc, dm), lambda i, pos: (i, 0)),
                       pl.BlockSpec((tc, dm), lambda i, pos: (i, 0))],
            scratch_shapes=[pltpu.VMEM((TOP_K, tc, dm), y_sorted.dtype),
                            pltpu.SemaphoreType.DMA(())],
        ),
        out_shape=[jax.ShapeDtypeStruct((t, dm), F32), jax.ShapeDtypeStruct((t, dm), BF16)],
        compiler_params=_params(("arbitrary",)),
        name="moe_combine_ln",
    )(pos, y_sorted, x, g.reshape(1, dm), b.reshape(1, dm))


def _routing_tables(top_idx, top_w, n_experts, tm):
    t = top_idx.shape[0]
    n_pairs = TOP_K * t
    n_rows = n_pairs + n_experts * tm
    n_tiles = n_rows // tm
    e_flat = top_idx.T.reshape(n_pairs)
    w_flat = top_w.T.reshape(n_pairs)
    tok_flat = jnp.tile(jnp.arange(t, dtype=jnp.int32), TOP_K)
    onehot = (e_flat[:, None] == jnp.arange(n_experts, dtype=jnp.int32)[None, :]).astype(jnp.int32)
    rank = jnp.sum((jnp.cumsum(onehot, axis=0) - onehot) * onehot, axis=1)
    counts = jnp.sum(onehot, axis=0)
    tiles_per = (counts + tm - 1) // tm
    tile_end = jnp.cumsum(tiles_per)
    row_start = (tile_end - tiles_per) * tm
    pos = (row_start[e_flat] + rank).astype(jnp.int32)
    row_token = jnp.zeros((n_rows,), jnp.int32).at[pos].set(tok_flat)
    row_gate = jnp.zeros((n_rows,), F32).at[pos].set(w_flat)
    tile_ids = jnp.arange(n_tiles, dtype=jnp.int32)
    total = tile_end[-1]
    tile_valid = (tile_ids < total).astype(jnp.int32)
    clipped = jnp.minimum(tile_ids, total - 1)
    tile_expert = jnp.minimum(jnp.searchsorted(tile_end, clipped, side="right"),
                              n_experts - 1).astype(jnp.int32)
    return pos, row_token, row_gate.reshape(n_rows, 1), tile_expert, tile_valid


def kernel(x, w_in, w_out, attn_lambda_q1, attn_lambda_k1, attn_lambda_q2, attn_lambda_k2, attn_subln_w, ssm_a_re, ssm_a_im, ssm_log_dt, ssm_b_re, ssm_b_im, ssm_c_re, ssm_c_im, ssm_d, ssm_w_glu, ln1_g, ln1_b, ln2_g, ln2_b, ffn_w_gate, ffn_w_up, ffn_w_down, moe_w_router, moe_w_gate, moe_w_up, moe_w_down):
    bsz, seq, dm = x.shape
    depth = w_in.shape[0]
    d = attn_lambda_q1.shape[-1]
    ssm_width = ssm_d.shape[-1]
    attn_width = w_out.shape[1] - ssm_width
    n_heads = attn_width // (2 * d)
    n_experts = moe_w_router.shape[-1]
    t = bsz * seq
    alpha = (2.0 * depth) ** 0.25
    moe_tm = _pick(t, 512, V7X_SUBLANES * 2)

    xf = x.reshape(t, dm).astype(F32)
    xb = xf.astype(BF16)
    col_scale = jnp.concatenate([jnp.full((1, attn_width), d ** -0.5, F32),
                                 jnp.ones((1, w_in.shape[2] - attn_width), F32)], axis=1)

    for l in range(depth):
        lam_init = 0.8 - 0.6 * math.exp(-0.3 * l)
        h = _matmul(xb, [w_in[l].astype(BF16)], _epi_scale_cols, BF16, tm=1024, tn=512,
                    col_extras=[col_scale], name=f"w_in_{l}")
        h = h.reshape(bsz, seq, -1)
        lam_params = jnp.stack([attn_lambda_q1[l], attn_lambda_k1[l],
                                attn_lambda_q2[l], attn_lambda_k2[l]]).astype(F32)
        o = _diff_attention(h, lam_params, attn_subln_w[l].astype(F32), n_heads=n_heads, d=d,
                            lam_init=lam_init, tq=512)
        g = _s5_gelu(h[:, :, 3 * attn_width:], ssm_a_re[l], ssm_a_im[l], ssm_log_dt[l],
                     ssm_b_re[l], ssm_b_im[l], ssm_c_re[l], ssm_c_im[l], ssm_d[l])
        g = g.reshape(t, ssm_width)
        y = _matmul(g, [ssm_w_glu[l].astype(BF16)], _epi_glu, BF16, tm=1024, tn=512,
                    row_extras=[g], name=f"ssm_glu_{l}")
        mix_in = jnp.concatenate([o.reshape(t, attn_width), y], axis=-1)
        xf, xb = _matmul_ln(mix_in, w_out[l].astype(BF16), xf, ln1_g[l].astype(F32),
                            ln1_b[l].astype(F32), alpha=alpha, tm=512, tk=512,
                            name=f"w_out_ln_{l}")
        if l % 2 == 0:
            e = l // 2
            hid = _matmul(xb, [ffn_w_gate[e].astype(BF16), ffn_w_up[e].astype(BF16)],
                          _epi_swiglu, BF16, tm=1024, tn=256, name=f"ffn_gate_up_{l}")
            xf, xb = _matmul_ln(hid, ffn_w_down[e].astype(BF16), xf, ln2_g[l].astype(F32),
                                ln2_b[l].astype(F32), alpha=alpha, tm=512, tk=512,
                                name=f"ffn_down_ln_{l}")
        else:
            e = l // 2
            top_idx, top_w = _router(xf, moe_w_router[e], tm=512)
            pos, row_token, row_gate, tile_expert, tile_valid = _routing_tables(
                top_idx, top_w, n_experts, moe_tm)
            xs = _gather_rows(xf, row_token, tg=256)
            hid = _grouped_matmul(xs, [moe_w_gate[e].astype(BF16), moe_w_up[e].astype(BF16)],
                                  tile_expert, tile_valid, _epi_swiglu, BF16, tm=moe_tm, tn=512,
                                  name=f"moe_gate_up_{l}")
            ys = _grouped_matmul(hid, [moe_w_down[e].astype(BF16)], tile_expert, tile_valid,
                                 _epi_row_scale, F32, tm=moe_tm, tn=1024,
                                 row_extras=[row_gate], name=f"moe_down_{l}")
            xf, xb = _combine_ln(ys, pos, xf, ln2_g[l].astype(F32), ln2_b[l].astype(F32),
                                 alpha=alpha, tc=256)
    return xf.reshape(bsz, seq, dm).astype(x.dtype)
```

```python
import functools
import math

import jax
import jax.numpy as jnp
from jax import lax
from jax.experimental import pallas as pl
from jax.experimental.pallas import tpu as pltpu

F32 = jnp.float32
BF16 = jnp.bfloat16

V7X_LANES = 128
V7X_SUBLANES = 8
V7X_VMEM_LIMIT_BYTES = 56 * 1024 * 1024

ATTN_CHUNK = 64
TOP_K = 2
LN_EPS = 1e-5
RMS_EPS = 1e-5
MASK_VALUE = -1e30
S5_CHUNK = 64
FFN_K_TILE = 512


def _pick(dim, target, align):
    if dim <= target:
        return dim
    t = (target // align) * align
    while t >= align:
        if dim % t == 0:
            return t
        t -= align
    raise ValueError(f"no tile for dim={dim} target={target} align={align}")


def _params(semantics):
    return pltpu.CompilerParams(dimension_semantics=semantics,
                                vmem_limit_bytes=V7X_VMEM_LIMIT_BYTES)


def _mm_kernel(*refs, n_rhs, n_row, n_col, epilogue):
    lhs_ref = refs[0]
    rhs_refs = refs[1:1 + n_rhs]
    row_refs = refs[1 + n_rhs:1 + n_rhs + n_row]
    col_refs = refs[1 + n_rhs + n_row:1 + n_rhs + n_row + n_col]
    o_ref = refs[-1]
    a = lhs_ref[...]
    accs = [jnp.dot(a, r[...], preferred_element_type=F32) for r in rhs_refs]
    out = epilogue(accs, [r[...] for r in row_refs], [c[...] for c in col_refs])
    o_ref[...] = out.astype(o_ref.dtype)


def _matmul(lhs, rhss, epilogue, out_dtype, *, tm, tn, row_extras=(), col_extras=(), name):
    m, k = lhs.shape
    n = rhss[0].shape[1]
    tm = _pick(m, tm, V7X_SUBLANES * 2)
    tn = _pick(n, tn, V7X_LANES)
    in_specs = [pl.BlockSpec((tm, k), lambda i, j: (i, 0))]
    in_specs += [pl.BlockSpec((k, tn), lambda i, j: (0, j)) for _ in rhss]
    in_specs += [pl.BlockSpec((tm, tn), lambda i, j: (i, j)) for _ in row_extras]
    in_specs += [pl.BlockSpec((1, tn), lambda i, j: (0, j)) for _ in col_extras]
    body = functools.partial(_mm_kernel, n_rhs=len(rhss), n_row=len(row_extras),
                             n_col=len(col_extras), epilogue=epilogue)
    return pl.pallas_call(
        body,
        grid=(m // tm, n // tn),
        in_specs=in_specs,
        out_specs=pl.BlockSpec((tm, tn), lambda i, j: (i, j)),
        out_shape=jax.ShapeDtypeStruct((m, n), out_dtype),
        compiler_params=_params(("parallel", "arbitrary")),
        name=name,
    )(lhs, *rhss, *row_extras, *col_extras)


def _epi_scale_cols(accs, rows, cols):
    return accs[0] * cols[0]


def _epi_swiglu(accs, rows, cols):
    g = accs[0]
    return g * jax.nn.sigmoid(g) * accs[1]


def _epi_glu(accs, rows, cols):
    g = rows[0].astype(F32)
    return g * jax.nn.sigmoid(accs[0])


def _layer_norm_rows(r, g, b):
    mu = jnp.mean(r, axis=-1, keepdims=True)
    c = r - mu
    var = jnp.mean(c * c, axis=-1, keepdims=True)
    return c * lax.rsqrt(var + LN_EPS) * g + b


def _mm_ln_kernel(*refs, alpha, part_blocks):
    n_parts = len(part_blocks)
    lhs_refs = refs[:n_parts]
    rhs_ref, x_ref, g_ref, b_ref, of_ref, ob_ref = refs[n_parts:]
    kk = pl.program_id(1)

    @pl.when(kk == 0)
    def _():
        of_ref[...] = jnp.zeros_like(of_ref)

    first = 0
    for lhs_ref, blocks in zip(lhs_refs, part_blocks):
        @pl.when(jnp.logical_and(kk >= first, kk < first + blocks))
        def _(lhs_ref=lhs_ref):
            of_ref[...] += jnp.dot(lhs_ref[...], rhs_ref[...], preferred_element_type=F32)
        first += blocks

    @pl.when(kk == pl.num_programs(1) - 1)
    def _():
        y = _layer_norm_rows(alpha * x_ref[...] + of_ref[...], g_ref[...], b_ref[...])
        of_ref[...] = y
        ob_ref[...] = y.astype(BF16)


def _matmul_ln(lhs_parts, rhs, x, g, b, *, alpha, tm, tk, name):
    m = lhs_parts[0].shape[0]
    n = rhs.shape[1]
    tm = _pick(m, tm, V7X_SUBLANES * 2)
    tk = _pick(math.gcd(*[p.shape[1] for p in lhs_parts]), tk, V7X_LANES)
    part_blocks = tuple(p.shape[1] // tk for p in lhs_parts)

    def part_spec(first, blocks):
        return pl.BlockSpec((tm, tk),
                            lambda i, kk: (i, jnp.clip(kk - first, 0, blocks - 1)))

    firsts = [sum(part_blocks[:p]) for p in range(len(part_blocks))]
    return pl.pallas_call(
        functools.partial(_mm_ln_kernel, alpha=alpha, part_blocks=part_blocks),
        grid=(m // tm, sum(part_blocks)),
        in_specs=[part_spec(f, nb) for f, nb in zip(firsts, part_blocks)] + [
            pl.BlockSpec((tk, n), lambda i, kk: (kk, 0)),
            pl.BlockSpec((tm, n), lambda i, kk: (i, 0), pipeline_mode=pl.Buffered(1)),
            pl.BlockSpec((1, n), lambda i, kk: (0, 0)),
            pl.BlockSpec((1, n), lambda i, kk: (0, 0)),
        ],
        out_specs=[pl.BlockSpec((tm, n), lambda i, kk: (i, 0)),
                   pl.BlockSpec((tm, n), lambda i, kk: (i, 0))],
        out_shape=[jax.ShapeDtypeStruct((m, n), F32), jax.ShapeDtypeStruct((m, n), BF16)],
        compiler_params=_params(("parallel", "arbitrary")),
        name=name,
    )(*lhs_parts, rhs, x, g.reshape(1, n), b.reshape(1, n))


def _attn_kernel(lam_ref, w_ref, q_ref, k_ref, v_ref, o_ref, m_sc, l_sc, acc_sc, *,
                 tq, tk, d, lam_init):
    qi = pl.program_id(2)
    hw = 2 * d
    lanes = V7X_LANES
    m_sc[...] = jnp.full(m_sc.shape, MASK_VALUE, F32)
    l_sc[...] = jnp.zeros_like(l_sc)
    acc_sc[...] = jnp.zeros_like(acc_sc)

    def step(j, diag_block):
        r0 = 0 if diag_block is None else diag_block * tk
        nr = tq - r0
        start = pl.multiple_of(j * tk, tk)
        kb = k_ref[pl.ds(start, tk), :]
        vb = v_ref[pl.ds(start, tk), :]
        if diag_block is not None:
            row = lax.broadcasted_iota(jnp.int32, (nr, tk), 0) // ATTN_CHUNK
            col = lax.broadcasted_iota(jnp.int32, (nr, tk), 1) // ATTN_CHUNK
            allowed = col <= row
        for mp in range(2):
            qm = q_ref[r0:tq, mp * d:(mp + 1) * d]
            km = kb[:, mp * d:(mp + 1) * d]
            s = lax.dot_general(qm, km, (((1,), (1,)), ((), ())), preferred_element_type=F32)
            if diag_block is not None:
                s = jnp.where(allowed, s, MASK_VALUE)
            m_prev = m_sc[mp, r0:tq]
            m_next = jnp.maximum(m_prev, jnp.max(s, axis=-1, keepdims=True))
            alpha = jnp.exp2(m_prev - m_next)
            ps = [jnp.exp2(s[:, c * lanes:(c + 1) * lanes] - m_next) for c in range(tk // lanes)]
            psum = ps[0]
            for pc in ps[1:]:
                psum = psum + pc
            l_sc[mp, r0:tq] = alpha * l_sc[mp, r0:tq] + psum
            p = jnp.concatenate(ps, axis=1).astype(BF16)
            alpha_w = jnp.concatenate([alpha] * (hw // lanes), axis=1)
            acc_sc[mp, r0:tq] = alpha_w * acc_sc[mp, r0:tq] + jnp.dot(
                p, vb, preferred_element_type=F32)
            m_sc[mp, r0:tq] = m_next

    def body(j, carry):
        step(j, None)
        return carry

    n_diag = tq // tk
    lax.fori_loop(0, qi * n_diag, body, 0)
    for c in range(n_diag):
        step(qi * n_diag + c, c)

    lam_p = lam_ref[...]
    lam = (jnp.exp(jnp.sum(lam_p[0:1] * lam_p[1:2], axis=-1, keepdims=True))
           - jnp.exp(jnp.sum(lam_p[2:3] * lam_p[3:4], axis=-1, keepdims=True)) + lam_init)
    l0 = jnp.sum(l_sc[0], axis=-1, keepdims=True)
    l1 = jnp.sum(l_sc[1], axis=-1, keepdims=True)
    o = acc_sc[0] / l0 - lam * (acc_sc[1] / l1)
    o = o * lax.rsqrt(jnp.mean(o * o, axis=-1, keepdims=True) + RMS_EPS)
    o_ref[...] = (o * w_ref[...] * (1.0 - lam_init)).astype(o_ref.dtype)


def _diff_attention(h, lam_params, subln_w, *, n_heads, d, lam_init, tq, tk):
    bsz, seq, _ = h.shape
    hw = 2 * d
    tq = _pick(seq, tq, V7X_LANES)
    tk = _pick(tq, tk, V7X_LANES)
    return pl.pallas_call(
        functools.partial(_attn_kernel, tq=tq, tk=tk, d=d, lam_init=lam_init),
        grid=(bsz, n_heads, seq // tq),
        in_specs=[
            pl.BlockSpec((4, d), lambda b, hd, i: (0, 0)),
            pl.BlockSpec((1, hw), lambda b, hd, i: (0, 0)),
            pl.BlockSpec((None, tq, hw), lambda b, hd, i: (b, i, hd)),
            pl.BlockSpec((None, seq, hw), lambda b, hd, i: (b, 0, n_heads + hd)),
            pl.BlockSpec((None, seq, hw), lambda b, hd, i: (b, 0, 2 * n_heads + hd)),
        ],
        out_specs=pl.BlockSpec((None, tq, hw), lambda b, hd, i: (b, i, hd)),
        out_shape=jax.ShapeDtypeStruct((bsz, seq, n_heads * hw), BF16),
        scratch_shapes=[pltpu.VMEM((2, tq, V7X_LANES), F32), pltpu.VMEM((2, tq, V7X_LANES), F32),
                        pltpu.VMEM((2, tq, hw), F32)],
        compiler_params=_params(("parallel", "parallel", "arbitrary")),
        name="diff_attention",
    )(lam_params, subln_w.reshape(1, hw), h, h, h)


def _s5_kernel(u_ref, acol_ref, arow_ref, bt_ref, bx_ref, ct_ref, d_ref, o_ref, m_ref, *,
               tc, p, n, n_chunks):
    tcp = tc * p
    u = u_ref[...]
    rows = u.shape[0]

    ar = acol_ref[:, 0:1]
    ai = acol_ref[:, 1:2]
    dt = acol_ref[:, 2:3]
    tau = (lax.broadcasted_iota(jnp.int32, (1, tcp), 1) // p).astype(F32)

    def power_table(t):
        mag = jnp.exp((ar * dt) * t)
        ang = (ai * dt) * t
        return mag * jnp.cos(ang), mag * jnp.sin(ang)

    mag1 = jnp.exp(ar * dt)
    lr = mag1 * jnp.cos(ai * dt)
    li = mag1 * jnp.sin(ai * dt)
    den = ar * ar + ai * ai
    f_re = ((lr - 1.0) * ar + li * ai) / den
    f_im = (li * ar - (lr - 1.0) * ai) / den
    bt_re = bt_ref[0]
    bt_im = bt_ref[1]
    bb_re = f_re * bt_re - f_im * bt_im
    bb_im = f_re * bt_im + f_im * bt_re

    pr_re, pr_im = power_table((tc - 1.0) - tau)
    z_re = pr_re * bb_re - pr_im * bb_im
    z_im = pr_re * bb_im + pr_im * bb_re
    wt = jnp.concatenate([z_re, z_im], axis=0).astype(BF16)

    pf_re, pf_im = power_table(tau)
    ct_re = ct_ref[0]
    ct_im = ct_ref[1]
    v_re = pf_re * ct_re - pf_im * ct_im
    v_im = pf_re * ct_im + pf_im * ct_re
    vmat = jnp.concatenate([v_re, -v_im], axis=0)

    ar2 = arow_ref[0:1, :]
    ai2 = arow_ref[1:2, :]
    dt2 = arow_ref[2:3, :]
    half = lax.broadcasted_iota(jnp.int32, (1, 2 * n), 1) < n
    sgn = jnp.where(half, -1.0, 1.0).astype(F32)
    mag1r = jnp.exp(ar2 * dt2)
    lr2 = mag1r * jnp.cos(ai2 * dt2)
    li2 = mag1r * jnp.sin(ai2 * dt2)
    den2 = ar2 * ar2 + ai2 * ai2
    fr2 = ((lr2 - 1.0) * ar2 + li2 * ai2) / den2
    fi2 = (li2 * ar2 - (lr2 - 1.0) * ai2) / den2
    bbt = fr2 * bx_ref[0] + (sgn * fi2) * bx_ref[1]
    kq = jnp.dot(bbt, vmat, preferred_element_type=F32, precision=lax.Precision.HIGHEST)

    lane = lax.broadcasted_iota(jnp.int32, (p, tcp), 1)
    m_ref[0:p, :] = kq.astype(BF16)
    for s in range(1, tc):
        blk = jnp.where(lane >= s * p, pltpu.roll(kq, s * p, axis=1), 0.0)
        m_ref[s * p:(s + 1) * p, :] = blk.astype(BF16)

    def cmul(x, r2, i2s):
        return x * r2 + pltpu.roll(x, n, axis=1) * i2s

    st = lax.dot_general(u, wt, (((1,), (1,)), ((), ())), preferred_element_type=F32)
    cidx = lax.broadcasted_iota(jnp.int32, (rows, 2 * n), 0) % n_chunks
    mag_c = jnp.exp(ar2 * dt2 * tc)
    ang_c = ai2 * dt2 * tc
    r2 = mag_c * jnp.cos(ang_c)
    i2 = sgn * mag_c * jnp.sin(ang_c)
    k = 1
    while k < n_chunks:
        sh = jnp.where(cidx >= k, pltpu.roll(st, k, axis=0), 0.0)
        st = st + cmul(sh, r2, i2)
        r2, i2 = r2 * r2 - i2 * i2, 2.0 * r2 * i2
        k *= 2
    hin = jnp.where(cidx >= 1, pltpu.roll(st, 1, axis=0), 0.0)
    hp = cmul(hin, lr2, sgn * li2)

    y = jnp.dot(u, m_ref[...], preferred_element_type=F32)
    y = y + jnp.dot(hp.astype(BF16), vmat.astype(BF16), preferred_element_type=F32)
    y = y + u.astype(F32) * d_ref[...]
    g = 0.5 * y * (1.0 + lax.erf(y * (2.0 ** -0.5)))
    o_ref[...] = g.astype(o_ref.dtype)


def _s5_gelu(u, a_re, a_im, log_dt, b_re, b_im, c_re, c_im, d_skip):
    bsz, seq, width = u.shape
    n_groups, n = a_re.shape
    p = b_re.shape[-1]
    tc = S5_CHUNK
    n_chunks = seq // tc
    rows = bsz * n_chunks
    tcp = tc * p
    uf = u.reshape(bsz, n_chunks, tc, n_groups, p)
    uf = jnp.transpose(uf, (3, 0, 1, 2, 4)).reshape(n_groups, rows, tcp)
    dt = jnp.exp(log_dt.astype(F32))
    dtb = jnp.broadcast_to(dt[:, None], (n_groups, n))
    acol = jnp.stack([a_re.astype(F32), a_im.astype(F32), dtb], axis=-1)
    arow = jnp.stack([jnp.tile(a_re.astype(F32), (1, 2)), jnp.tile(a_im.astype(F32), (1, 2)),
                      jnp.tile(dtb, (1, 2))], axis=1)
    bt = jnp.stack([jnp.tile(b_re.astype(F32), (1, 1, tc)),
                    jnp.tile(b_im.astype(F32), (1, 1, tc))], axis=1)
    btr = jnp.swapaxes(b_re.astype(F32), 1, 2)
    bti = jnp.swapaxes(b_im.astype(F32), 1, 2)
    bx = jnp.stack([jnp.concatenate([btr, bti], axis=-1),
                    jnp.concatenate([bti, btr], axis=-1)], axis=1)
    ctr = jnp.swapaxes(c_re.astype(F32), 1, 2)
    cti = jnp.swapaxes(c_im.astype(F32), 1, 2)
    ct = jnp.stack([jnp.tile(ctr, (1, 1, tc)), jnp.tile(cti, (1, 1, tc))], axis=1)
    dtile = jnp.tile(d_skip.astype(F32).reshape(n_groups, 1, p), (1, 1, tc))

    out = pl.pallas_call(
        functools.partial(_s5_kernel, tc=tc, p=p, n=n, n_chunks=n_chunks),
        grid=(n_groups,),
        in_specs=[
            pl.BlockSpec((None, rows, tcp), lambda g: (g, 0, 0)),
            pl.BlockSpec((None, n, 3), lambda g: (g, 0, 0)),
            pl.BlockSpec((None, 3, 2 * n), lambda g: (g, 0, 0)),
            pl.BlockSpec((None, 2, n, tcp), lambda g: (g, 0, 0, 0)),
            pl.BlockSpec((None, 2, p, 2 * n), lambda g: (g, 0, 0, 0)),
            pl.BlockSpec((None, 2, n, tcp), lambda g: (g, 0, 0, 0)),
            pl.BlockSpec((None, 1, tcp), lambda g: (g, 0, 0)),
        ],
        out_specs=pl.BlockSpec((None, rows, tcp), lambda g: (g, 0, 0)),
        out_shape=jax.ShapeDtypeStruct((n_groups, rows, tcp), BF16),
        scratch_shapes=[pltpu.VMEM((tcp, tcp), BF16)],
        compiler_params=_params(("parallel",)),
        name="s5_gelu",
    )(uf, acol, arow, bt, bx, ct, dtile)
    out = out.reshape(n_groups, bsz, n_chunks, tc, p)
    return jnp.transpose(out, (1, 2, 3, 0, 4)).reshape(bsz, seq, width)


def _router_kernel(x_ref, w_ref, idx_ref, gate_ref, *, n_experts):
    logits = jnp.dot(x_ref[...], w_ref[...], preferred_element_type=F32,
                     precision=lax.Precision.HIGHEST)
    col = lax.broadcasted_iota(jnp.int32, logits.shape, 1)
    big = jnp.int32(logits.shape[1])
    lg = jnp.where(col < n_experts, logits, -jnp.inf)
    m1 = jnp.max(lg, axis=-1, keepdims=True)
    i1 = jnp.min(jnp.where(lg == m1, col, big), axis=-1, keepdims=True)
    lg2 = jnp.where(col == i1, -jnp.inf, lg)
    m2 = jnp.max(lg2, axis=-1, keepdims=True)
    i2 = jnp.min(jnp.where(lg2 == m2, col, big), axis=-1, keepdims=True)
    e = jnp.exp(m2 - m1)
    w1 = 1.0 / (1.0 + e)
    w2 = e / (1.0 + e)
    idx_ref[...] = jnp.where(col == 0, i1, jnp.where(col == 1, i2, 0))
    gate_ref[...] = jnp.where(col == 0, w1, jnp.where(col == 1, w2, 0.0))


def _router(x, w_router, *, tm):
    t, dm = x.shape
    n_experts = w_router.shape[1]
    wp = jnp.zeros((dm, V7X_LANES), F32).at[:, :n_experts].set(w_router.astype(F32))
    tm = _pick(t, tm, V7X_SUBLANES)
    idx, gate = pl.pallas_call(
        functools.partial(_router_kernel, n_experts=n_experts),
        grid=(t // tm,),
        in_specs=[pl.BlockSpec((tm, dm), lambda i: (i, 0)),
                  pl.BlockSpec((dm, V7X_LANES), lambda i: (0, 0))],
        out_specs=[pl.BlockSpec((tm, V7X_LANES), lambda i: (i, 0)),
                   pl.BlockSpec((tm, V7X_LANES), lambda i: (i, 0))],
        out_shape=[jax.ShapeDtypeStruct((t, V7X_LANES), jnp.int32),
                   jax.ShapeDtypeStruct((t, V7X_LANES), F32)],
        compiler_params=_params(("parallel",)),
        name="moe_router",
    )(x, wp)
    return idx[:, :TOP_K], gate[:, :TOP_K]


GATHER_UNROLL = 8


def _moe_gate_up_kernel(te_ref, tv_ref, tok_ref, x_hbm, wg_ref, wu_ref, o_ref, xbuf, xb, sem, *,
                        tm):
    i = pl.program_id(0)
    j = pl.program_id(1)
    n_tiles = pl.num_programs(0)
    slot = i % 2
    valid = tv_ref[i] != 0

    def issue_tile(tile, dst_slot):
        def issue(r, carry):
            pltpu.make_async_copy(x_hbm.at[pl.ds(tok_ref[tile * tm + r], 1), :],
                                  xbuf.at[dst_slot, pl.ds(r, 1), :], sem.at[dst_slot]).start()
            return carry
        lax.fori_loop(0, tm, issue, 0, unroll=GATHER_UNROLL)

    def wait_tile(dst_slot):
        pltpu.make_async_copy(x_hbm.at[pl.ds(0, tm), :], xbuf.at[dst_slot],
                              sem.at[dst_slot]).wait()

    @pl.when(jnp.logical_and(j == 0, i == 0))
    def _():
        issue_tile(0, 0)

    @pl.when(jnp.logical_and(j == 0, valid))
    def _():
        wait_tile(slot)

    nxt = jnp.minimum(i + 1, n_tiles - 1)

    @pl.when(jnp.logical_and(j == 0, jnp.logical_and(i + 1 < n_tiles, tv_ref[nxt] != 0)))
    def _():
        issue_tile(i + 1, 1 - slot)

    @pl.when(jnp.logical_and(j == 0, valid))
    def _():
        xb[...] = xbuf[slot].astype(BF16)

    @pl.when(valid)
    def _():
        a = xb[...]
        gate = jnp.dot(a, wg_ref[...], preferred_element_type=F32)
        up = jnp.dot(a, wu_ref[...], preferred_element_type=F32)
        o_ref[...] = _epi_swiglu([gate, up], [], []).astype(o_ref.dtype)

    @pl.when(jnp.logical_not(valid))
    def _():
        o_ref[...] = jnp.zeros_like(o_ref)


def _moe_gate_up(x, w_gate, w_up, row_token, tile_expert, tile_valid, *, tm, tn):
    k = x.shape[1]
    n_rows = row_token.shape[0]
    n = w_gate.shape[2]
    tn = _pick(n, tn, V7X_LANES)
    return pl.pallas_call(
        functools.partial(_moe_gate_up_kernel, tm=tm),
        grid_spec=pltpu.PrefetchScalarGridSpec(
            num_scalar_prefetch=3,
            grid=(n_rows // tm, n // tn),
            in_specs=[pl.BlockSpec(memory_space=pl.ANY),
                      pl.BlockSpec((None, k, tn), lambda i, j, te, tv, tok: (te[i], 0, j)),
                      pl.BlockSpec((None, k, tn), lambda i, j, te, tv, tok: (te[i], 0, j))],
            out_specs=pl.BlockSpec((tm, tn), lambda i, j, te, tv, tok: (i, j)),
            scratch_shapes=[pltpu.VMEM((2, tm, k), x.dtype), pltpu.VMEM((tm, k), BF16),
                            pltpu.SemaphoreType.DMA((2,))],
        ),
        out_shape=jax.ShapeDtypeStruct((n_rows, n), BF16),
        compiler_params=_params(("arbitrary", "arbitrary")),
        name="moe_gate_up",
    )(tile_expert, tile_valid, row_token, x, w_gate, w_up)


def _grouped_kernel(te_ref, tv_ref, *refs, n_rhs, n_row, epilogue):
    lhs_ref = refs[0]
    rhs_refs = refs[1:1 + n_rhs]
    row_refs = refs[1 + n_rhs:1 + n_rhs + n_row]
    o_ref = refs[-1]

    valid = tv_ref[pl.program_id(0)] != 0

    @pl.when(valid)
    def _():
        a = lhs_ref[...]
        accs = [jnp.dot(a, r[...], preferred_element_type=F32) for r in rhs_refs]
        o_ref[...] = epilogue(accs, [r[...] for r in row_refs], []).astype(o_ref.dtype)

    @pl.when(jnp.logical_not(valid))
    def _():
        o_ref[...] = jnp.zeros_like(o_ref)


def _grouped_matmul(lhs, rhss, tile_expert, tile_valid, epilogue, out_dtype, *, tm, tn,
                    row_extras=(), name):
    r, k = lhs.shape
    n = rhss[0].shape[2]
    tn = _pick(n, tn, V7X_LANES)
    in_specs = [pl.BlockSpec((tm, k), lambda i, j, te, tv: (i, 0))]
    in_specs += [pl.BlockSpec((None, k, tn), lambda i, j, te, tv: (te[i], 0, j)) for _ in rhss]
    in_specs += [pl.BlockSpec((tm, 1), lambda i, j, te, tv: (i, 0)) for _ in row_extras]
    return pl.pallas_call(
        functools.partial(_grouped_kernel, n_rhs=len(rhss), n_row=len(row_extras),
                          epilogue=epilogue),
        grid_spec=pltpu.PrefetchScalarGridSpec(
            num_scalar_prefetch=2,
            grid=(r // tm, n // tn),
            in_specs=in_specs,
            out_specs=pl.BlockSpec((tm, tn), lambda i, j, te, tv: (i, j)),
        ),
        out_shape=jax.ShapeDtypeStruct((r, n), out_dtype),
        compiler_params=_params(("parallel", "arbitrary")),
        name=name,
    )(tile_expert, tile_valid, lhs, *rhss, *row_extras)


def _epi_row_scale(accs, rows, cols):
    return accs[0] * rows[0]


def _combine_ln_kernel(pos_ref, y_hbm, x_ref, g_ref, b_ref, of_ref, ob_ref, buf, sem, *,
                       tc, n_tokens, alpha):
    base = pl.program_id(0) * tc

    def issue(r, carry):
        for kk in range(TOP_K):
            pltpu.make_async_copy(y_hbm.at[pl.ds(pos_ref[kk * n_tokens + base + r], 1), :],
                                  buf.at[kk, pl.ds(r, 1), :], sem).start()
        return carry

    lax.fori_loop(0, tc, issue, 0, unroll=GATHER_UNROLL)
    for kk in range(TOP_K):
        pltpu.make_async_copy(y_hbm.at[pl.ds(0, tc), :], buf.at[kk], sem).wait()
    f = buf[0]
    for kk in range(1, TOP_K):
        f = f + buf[kk]
    y = _layer_norm_rows(alpha * x_ref[...] + f, g_ref[...], b_ref[...])
    of_ref[...] = y
    ob_ref[...] = y.astype(BF16)


def _combine_ln(y_sorted, pos, x, g, b, *, alpha, tc):
    t, dm = x.shape
    tc = _pick(t, tc, V7X_SUBLANES * 2)
    return pl.pallas_call(
        functools.partial(_combine_ln_kernel, tc=tc, n_tokens=t, alpha=alpha),
        grid_spec=pltpu.PrefetchScalarGridSpec(
            num_scalar_prefetch=1,
            grid=(t // tc,),
            in_specs=[pl.BlockSpec(memory_space=pl.ANY),
                      pl.BlockSpec((tc, dm), lambda i, pos: (i, 0)),
                      pl.BlockSpec((1, dm), lambda i, pos: (0, 0)),
                      pl.BlockSpec((1, dm), lambda i, pos: (0, 0))],
            out_specs=[pl.BlockSpec((tc, dm), lambda i, pos: (i, 0)),
                       pl.BlockSpec((tc, dm), lambda i, pos: (i, 0))],
            scratch_shapes=[pltpu.VMEM((TOP_K, tc, dm), y_sorted.dtype),
                            pltpu.SemaphoreType.DMA(())],
        ),
        out_shape=[jax.ShapeDtypeStruct((t, dm), F32), jax.ShapeDtypeStruct((t, dm), BF16)],
        compiler_params=_params(("arbitrary",)),
        name="moe_combine_ln",
    )(pos, y_sorted, x, g.reshape(1, dm), b.reshape(1, dm))


def _routing_tables(top_idx, top_w, n_experts, tm):
    t = top_idx.shape[0]
    n_pairs = TOP_K * t
    n_rows = n_pairs + n_experts * tm
    n_tiles = n_rows // tm
    e_flat = top_idx.T.reshape(n_pairs)
    w_flat = top_w.T.reshape(n_pairs)
    tok_flat = jnp.tile(jnp.arange(t, dtype=jnp.int32), TOP_K)
    onehot = (e_flat[:, None] == jnp.arange(n_experts, dtype=jnp.int32)[None, :]).astype(jnp.int32)
    rank = jnp.sum((jnp.cumsum(onehot, axis=0) - onehot) * onehot, axis=1)
    counts = jnp.sum(onehot, axis=0)
    tiles_per = (counts + tm - 1) // tm
    tile_end = jnp.cumsum(tiles_per)
    row_start = (tile_end - tiles_per) * tm
    pos = (row_start[e_flat] + rank).astype(jnp.int32)
    row_token = jnp.zeros((n_rows,), jnp.int32).at[pos].set(tok_flat)
    row_gate = jnp.zeros((n_rows,), F32).at[pos].set(w_flat)
    tile_ids = jnp.arange(n_tiles, dtype=jnp.int32)
    total = tile_end[-1]
    tile_valid = (tile_ids < total).astype(jnp.int32)
    clipped = jnp.minimum(tile_ids, total - 1)
    tile_expert = jnp.minimum(
        jnp.sum((tile_end[None, :] <= clipped[:, None]).astype(jnp.int32), axis=1),
        n_experts - 1).astype(jnp.int32)
    return pos, row_token, row_gate.reshape(n_rows, 1), tile_expert, tile_valid


def kernel(x, w_in, w_out, attn_lambda_q1, attn_lambda_k1, attn_lambda_q2, attn_lambda_k2, attn_subln_w, ssm_a_re, ssm_a_im, ssm_log_dt, ssm_b_re, ssm_b_im, ssm_c_re, ssm_c_im, ssm_d, ssm_w_glu, ln1_g, ln1_b, ln2_g, ln2_b, ffn_w_gate, ffn_w_up, ffn_w_down, moe_w_router, moe_w_gate, moe_w_up, moe_w_down):
    bsz, seq, dm = x.shape
    depth = w_in.shape[0]
    d = attn_lambda_q1.shape[-1]
    ssm_width = ssm_d.shape[-1]
    attn_width = w_out.shape[1] - ssm_width
    n_heads = attn_width // (2 * d)
    n_experts = moe_w_router.shape[-1]
    t = bsz * seq
    alpha = (2.0 * depth) ** 0.25
    moe_tm = _pick(t, 512, V7X_SUBLANES * 2)

    xf = x.reshape(t, dm).astype(F32)
    xb = xf.astype(BF16)
    col_scale = jnp.concatenate([jnp.full((1, attn_width), d ** -0.5 * math.log2(math.e), F32),
                                 jnp.ones((1, w_in.shape[2] - attn_width), F32)], axis=1)

    for l in range(depth):
        lam_init = 0.8 - 0.6 * math.exp(-0.3 * l)
        h = _matmul(xb, [w_in[l].astype(BF16)], _epi_scale_cols, BF16, tm=1024, tn=512,
                    col_extras=[col_scale], name=f"w_in_{l}")
        h = h.reshape(bsz, seq, -1)
        lam_params = jnp.stack([attn_lambda_q1[l], attn_lambda_k1[l],
                                attn_lambda_q2[l], attn_lambda_k2[l]]).astype(F32)
        o = _diff_attention(h, lam_params, attn_subln_w[l].astype(F32), n_heads=n_heads, d=d,
                            lam_init=lam_init, tq=1024, tk=512)
        g = _s5_gelu(h[:, :, 3 * attn_width:], ssm_a_re[l], ssm_a_im[l], ssm_log_dt[l],
                     ssm_b_re[l], ssm_b_im[l], ssm_c_re[l], ssm_c_im[l], ssm_d[l])
        g = g.reshape(t, ssm_width)
        y = _matmul(g, [ssm_w_glu[l].astype(BF16)], _epi_glu, BF16, tm=1024, tn=512,
                    row_extras=[g], name=f"ssm_glu_{l}")
        xf, xb = _matmul_ln([o.reshape(t, attn_width), y], w_out[l].astype(BF16), xf,
                            ln1_g[l].astype(F32), ln1_b[l].astype(F32), alpha=alpha,
                            tm=512, tk=512, name=f"w_out_ln_{l}")
        if l % 2 == 0:
            e = l // 2
            pad = (-ffn_w_gate.shape[-1]) % FFN_K_TILE
            w_gate = jnp.pad(ffn_w_gate[e].astype(BF16), ((0, 0), (0, pad)))
            w_up = jnp.pad(ffn_w_up[e].astype(BF16), ((0, 0), (0, pad)))
            w_down = jnp.pad(ffn_w_down[e].astype(BF16), ((0, pad), (0, 0)))
            hid = _matmul(xb, [w_gate, w_up], _epi_swiglu, BF16, tm=1024, tn=256,
                          name=f"ffn_gate_up_{l}")
            xf, xb = _matmul_ln([hid], w_down, xf, ln2_g[l].astype(F32),
                                ln2_b[l].astype(F32), alpha=alpha, tm=512, tk=FFN_K_TILE,
                                name=f"ffn_down_ln_{l}")
        else:
            e = l // 2
            top_idx, top_w = _router(xf, moe_w_router[e], tm=512)
            pos, row_token, row_gate, tile_expert, tile_valid = _routing_tables(
                top_idx, top_w, n_experts, moe_tm)
            hid = _moe_gate_up(xf, moe_w_gate[e].astype(BF16), moe_w_up[e].astype(BF16),
                               row_token, tile_expert, tile_valid, tm=moe_tm, tn=512)
            ys = _grouped_matmul(hid, [moe_w_down[e].astype(BF16)], tile_expert, tile_valid,
                                 _epi_row_scale, F32, tm=moe_tm, tn=1024,
                                 row_extras=[row_gate], name=f"moe_down_{l}")
            xf, xb = _combine_ln(ys, pos, xf, ln2_g[l].astype(F32), ln2_b[l].astype(F32),
                                 alpha=alpha, tc=256)
    return xf.reshape(bsz, seq, dm).astype(x.dtype)
```

```python
import functools
import math

import jax
import jax.numpy as jnp
from jax import lax
from jax.experimental import pallas as pl
from jax.experimental.pallas import tpu as pltpu

F32 = jnp.float32
BF16 = jnp.bfloat16

V7X_LANES = 128
V7X_SUBLANES = 8
V7X_VMEM_LIMIT_BYTES = 56 * 1024 * 1024

ATTN_CHUNK = 64
TOP_K = 2
LN_EPS = 1e-5
RMS_EPS = 1e-5
MASK_VALUE = -1e30
S5_CHUNK = 64
FFN_K_TILE = 512


def _pick(dim, target, align):
    if dim <= target:
        return dim
    t = (target // align) * align
    while t >= align:
        if dim % t == 0:
            return t
        t -= align
    raise ValueError(f"no tile for dim={dim} target={target} align={align}")


def _params(semantics):
    return pltpu.CompilerParams(dimension_semantics=semantics,
                                vmem_limit_bytes=V7X_VMEM_LIMIT_BYTES)


def _mm_kernel(*refs, n_rhs, n_row, n_col, epilogue):
    lhs_ref = refs[0]
    rhs_refs = refs[1:1 + n_rhs]
    row_refs = refs[1 + n_rhs:1 + n_rhs + n_row]
    col_refs = refs[1 + n_rhs + n_row:1 + n_rhs + n_row + n_col]
    o_ref = refs[-1]
    a = lhs_ref[...]
    accs = [jnp.dot(a, r[...], preferred_element_type=F32) for r in rhs_refs]
    out = epilogue(accs, [r[...] for r in row_refs], [c[...] for c in col_refs])
    o_ref[...] = out.astype(o_ref.dtype)


def _matmul(lhs, rhss, epilogue, out_dtype, *, tm, tn, row_extras=(), col_extras=(), name):
    m, k = lhs.shape
    n = rhss[0].shape[1]
    tm = _pick(m, tm, V7X_SUBLANES * 2)
    tn = _pick(n, tn, V7X_LANES)
    in_specs = [pl.BlockSpec((tm, k), lambda i, j: (i, 0))]
    in_specs += [pl.BlockSpec((k, tn), lambda i, j: (0, j)) for _ in rhss]
    in_specs += [pl.BlockSpec((tm, tn), lambda i, j: (i, j)) for _ in row_extras]
    in_specs += [pl.BlockSpec((1, tn), lambda i, j: (0, j)) for _ in col_extras]
    body = functools.partial(_mm_kernel, n_rhs=len(rhss), n_row=len(row_extras),
                             n_col=len(col_extras), epilogue=epilogue)
    return pl.pallas_call(
        body,
        grid=(m // tm, n // tn),
        in_specs=in_specs,
        out_specs=pl.BlockSpec((tm, tn), lambda i, j: (i, j)),
        out_shape=jax.ShapeDtypeStruct((m, n), out_dtype),
        compiler_params=_params(("parallel", "arbitrary")),
        name=name,
    )(lhs, *rhss, *row_extras, *col_extras)


def _epi_scale_cols(accs, rows, cols):
    return accs[0] * cols[0]


def _epi_swiglu(accs, rows, cols):
    g = accs[0]
    return g * jax.nn.sigmoid(g) * accs[1]


def _epi_glu(accs, rows, cols):
    g = rows[0].astype(F32)
    return g * jax.nn.sigmoid(accs[0])


def _layer_norm_rows(r, g, b):
    mu = jnp.mean(r, axis=-1, keepdims=True)
    c = r - mu
    var = jnp.mean(c * c, axis=-1, keepdims=True)
    return c * lax.rsqrt(var + LN_EPS) * g + b


def _mm_ln_kernel(*refs, alpha, part_blocks):
    n_parts = len(part_blocks)
    lhs_refs = refs[:n_parts]
    rhs_ref, x_ref, g_ref, b_ref, of_ref, ob_ref = refs[n_parts:]
    kk = pl.program_id(1)

    @pl.when(kk == 0)
    def _():
        of_ref[...] = jnp.dot(lhs_refs[0][...], rhs_ref[...], preferred_element_type=F32)

    first = 0
    for lhs_ref, blocks in zip(lhs_refs, part_blocks):
        @pl.when(jnp.logical_and(kk >= max(first, 1), kk < first + blocks))
        def _(lhs_ref=lhs_ref):
            of_ref[...] += jnp.dot(lhs_ref[...], rhs_ref[...], preferred_element_type=F32)
        first += blocks

    @pl.when(kk == pl.num_programs(1) - 1)
    def _():
        y = _layer_norm_rows(alpha * x_ref[...] + of_ref[...], g_ref[...], b_ref[...])
        of_ref[...] = y
        ob_ref[...] = y.astype(BF16)


def _matmul_ln(lhs_parts, rhs, x, g, b, *, alpha, tm, tk, name):
    m = lhs_parts[0].shape[0]
    n = rhs.shape[1]
    tm = _pick(m, tm, V7X_SUBLANES * 2)
    tk = _pick(math.gcd(*[p.shape[1] for p in lhs_parts]), tk, V7X_LANES)
    part_blocks = tuple(p.shape[1] // tk for p in lhs_parts)

    def part_spec(first, blocks):
        return pl.BlockSpec((tm, tk),
                            lambda i, kk: (i, jnp.clip(kk - first, 0, blocks - 1)))

    firsts = [sum(part_blocks[:p]) for p in range(len(part_blocks))]
    return pl.pallas_call(
        functools.partial(_mm_ln_kernel, alpha=alpha, part_blocks=part_blocks),
        grid=(m // tm, sum(part_blocks)),
        in_specs=[part_spec(f, nb) for f, nb in zip(firsts, part_blocks)] + [
            pl.BlockSpec((tk, n), lambda i, kk: (kk, 0)),
            pl.BlockSpec((tm, n), lambda i, kk: (i, 0), pipeline_mode=pl.Buffered(1)),
            pl.BlockSpec((1, n), lambda i, kk: (0, 0)),
            pl.BlockSpec((1, n), lambda i, kk: (0, 0)),
        ],
        out_specs=[pl.BlockSpec((tm, n), lambda i, kk: (i, 0)),
                   pl.BlockSpec((tm, n), lambda i, kk: (i, 0))],
        out_shape=[jax.ShapeDtypeStruct((m, n), F32), jax.ShapeDtypeStruct((m, n), BF16)],
        compiler_params=_params(("parallel", "arbitrary")),
        name=name,
    )(*lhs_parts, rhs, x, g.reshape(1, n), b.reshape(1, n))


def _attn_kernel(lam_ref, w_ref, q_ref, k_ref, v_ref, o_ref, m_sc, l_sc, acc_sc, *,
                 tq, tk, tk_diag, d, lam_init):
    qi = pl.program_id(2)
    hw = 2 * d
    lanes = V7X_LANES
    m_sc[...] = jnp.full(m_sc.shape, MASK_VALUE, F32)
    l_sc[...] = jnp.zeros_like(l_sc)
    acc_sc[...] = jnp.zeros_like(acc_sc)

    def step(start, width, diag_block):
        r0 = 0 if diag_block is None else diag_block * width
        nr = tq - r0
        start = pl.multiple_of(start, width)
        kb = k_ref[pl.ds(start, width), :]
        vb = v_ref[pl.ds(start, width), :]
        if diag_block is not None:
            row = lax.broadcasted_iota(jnp.int32, (nr, width), 0) // ATTN_CHUNK
            col = lax.broadcasted_iota(jnp.int32, (nr, width), 1) // ATTN_CHUNK
            allowed = col <= row
        scores = [lax.dot_general(q_ref[r0:tq, mp * d:(mp + 1) * d], kb[:, mp * d:(mp + 1) * d],
                                  (((1,), (1,)), ((), ())), preferred_element_type=F32)
                  for mp in range(2)]
        for mp in range(2):
            s = scores[mp]
            if diag_block is not None:
                s = jnp.where(allowed, s, MASK_VALUE)
            m_prev = m_sc[mp, r0:tq]
            m_next = jnp.maximum(m_prev, jnp.max(s, axis=-1, keepdims=True))
            alpha = jnp.exp2(m_prev - m_next)
            ps = [jnp.exp2(s[:, c * lanes:(c + 1) * lanes] - m_next) for c in range(width // lanes)]
            psum = ps[0]
            for pc in ps[1:]:
                psum = psum + pc
            l_sc[mp, r0:tq] = alpha * l_sc[mp, r0:tq] + psum
            p = jnp.concatenate(ps, axis=1).astype(BF16)
            alpha_w = jnp.concatenate([alpha] * (hw // lanes), axis=1)
            acc_sc[mp, r0:tq] = alpha_w * acc_sc[mp, r0:tq] + jnp.dot(
                p, vb, preferred_element_type=F32)
            m_sc[mp, r0:tq] = m_next

    def body(j, carry):
        step(j * tk, tk, None)
        return carry

    lax.fori_loop(0, qi * (tq // tk), body, 0)
    for c in range(tq // tk_diag):
        step(qi * tq + c * tk_diag, tk_diag, c)

    lam_p = lam_ref[...]
    lam = (jnp.exp(jnp.sum(lam_p[0:1] * lam_p[1:2], axis=-1, keepdims=True))
           - jnp.exp(jnp.sum(lam_p[2:3] * lam_p[3:4], axis=-1, keepdims=True)) + lam_init)
    l0 = jnp.sum(l_sc[0], axis=-1, keepdims=True)
    l1 = jnp.sum(l_sc[1], axis=-1, keepdims=True)
    o = acc_sc[0] / l0 - lam * (acc_sc[1] / l1)
    o = o * lax.rsqrt(jnp.mean(o * o, axis=-1, keepdims=True) + RMS_EPS)
    o_ref[...] = (o * w_ref[...] * (1.0 - lam_init)).astype(o_ref.dtype)


def _diff_attention(h, lam_params, subln_w, *, n_heads, d, lam_init, tq, tk, tk_diag):
    bsz, seq, _ = h.shape
    hw = 2 * d
    tq = _pick(seq, tq, V7X_LANES)
    tk = _pick(tq, tk, V7X_LANES)
    tk_diag = _pick(tq, tk_diag, V7X_LANES)
    return pl.pallas_call(
        functools.partial(_attn_kernel, tq=tq, tk=tk, tk_diag=tk_diag, d=d, lam_init=lam_init),
        grid=(bsz, n_heads, seq // tq),
        in_specs=[
            pl.BlockSpec((4, d), lambda b, hd, i: (0, 0)),
            pl.BlockSpec((1, hw), lambda b, hd, i: (0, 0)),
            pl.BlockSpec((None, tq, hw), lambda b, hd, i: (b, i, hd)),
            pl.BlockSpec((None, seq, hw), lambda b, hd, i: (b, 0, n_heads + hd)),
            pl.BlockSpec((None, seq, hw), lambda b, hd, i: (b, 0, 2 * n_heads + hd)),
        ],
        out_specs=pl.BlockSpec((None, tq, hw), lambda b, hd, i: (b, i, hd)),
        out_shape=jax.ShapeDtypeStruct((bsz, seq, n_heads * hw), BF16),
        scratch_shapes=[pltpu.VMEM((2, tq, V7X_LANES), F32), pltpu.VMEM((2, tq, V7X_LANES), F32),
                        pltpu.VMEM((2, tq, hw), F32)],
        compiler_params=_params(("parallel", "parallel", "arbitrary")),
        name="diff_attention",
    )(lam_params, subln_w.reshape(1, hw), h, h, h)


def _s5_kernel(u_ref, acol_ref, arow_ref, bt_ref, bx_ref, ct_ref, d_ref, o_ref, m_ref, *,
               tc, p, n, n_chunks):
    tcp = tc * p
    u = u_ref[...]
    rows = u.shape[0]

    ar = acol_ref[:, 0:1]
    ai = acol_ref[:, 1:2]
    dt = acol_ref[:, 2:3]
    lanes = V7X_LANES
    spb = lanes // p
    n_blk = tcp // lanes
    lane_id = lax.broadcasted_iota(jnp.int32, (1, lanes), 1)
    tau0 = (lane_id // p).astype(F32)

    def powers(t):
        mag = jnp.exp((ar * dt) * t)
        ang = (ai * dt) * t
        return mag * jnp.cos(ang), mag * jnp.sin(ang)

    def cmul2(a_re, a_im, b_re, b_im):
        return a_re * b_re - a_im * b_im, a_re * b_im + a_im * b_re

    mag1 = jnp.exp(ar * dt)
    lr = mag1 * jnp.cos(ai * dt)
    li = mag1 * jnp.sin(ai * dt)
    den = ar * ar + ai * ai
    f_re = ((lr - 1.0) * ar + li * ai) / den
    f_im = (li * ar - (lr - 1.0) * ai) / den
    bb_re, bb_im = cmul2(f_re, f_im, bt_ref[0], bt_ref[1])

    blk_re, blk_im = powers((lane_id * spb).astype(F32))
    zb_re, zb_im = cmul2(*powers((spb - 1.0) - tau0), bb_re, bb_im)
    vb_re, vb_im = cmul2(*powers(tau0), ct_ref[0], ct_ref[1])
    z_blocks, v_blocks = [], []
    for k in range(n_blk):
        kr = n_blk - 1 - k
        z_blocks.append(cmul2(zb_re, zb_im, blk_re[:, kr:kr + 1], blk_im[:, kr:kr + 1]))
        v_blocks.append(cmul2(vb_re, vb_im, blk_re[:, k:k + 1], blk_im[:, k:k + 1]))
    z_re = jnp.concatenate([z[0] for z in z_blocks], axis=1)
    z_im = jnp.concatenate([z[1] for z in z_blocks], axis=1)
    wt = jnp.concatenate([z_re, z_im], axis=0).astype(BF16)
    v_re = jnp.concatenate([v[0] for v in v_blocks], axis=1)
    v_im = jnp.concatenate([v[1] for v in v_blocks], axis=1)
    vmat = jnp.concatenate([v_re, -v_im], axis=0)

    ar2 = arow_ref[0:1, :]
    ai2 = arow_ref[1:2, :]
    dt2 = arow_ref[2:3, :]
    half = lax.broadcasted_iota(jnp.int32, (1, 2 * n), 1) < n
    sgn = jnp.where(half, -1.0, 1.0).astype(F32)
    mag1r = jnp.exp(ar2 * dt2)
    lr2 = mag1r * jnp.cos(ai2 * dt2)
    li2 = mag1r * jnp.sin(ai2 * dt2)
    den2 = ar2 * ar2 + ai2 * ai2
    fr2 = ((lr2 - 1.0) * ar2 + li2 * ai2) / den2
    fi2 = (li2 * ar2 - (lr2 - 1.0) * ai2) / den2
    bbt = fr2 * bx_ref[0] + (sgn * fi2) * bx_ref[1]
    kq = jnp.dot(bbt, vmat, preferred_element_type=F32, precision=lax.Precision.HIGHEST)

    lane = lax.broadcasted_iota(jnp.int32, (p, tcp), 1)
    m_ref[0:p, :] = kq.astype(BF16)
    for s in range(1, tc):
        blk = jnp.where(lane >= s * p, pltpu.roll(kq, s * p, axis=1), 0.0)
        m_ref[s * p:(s + 1) * p, :] = blk.astype(BF16)

    def cmul(x, r2, i2s):
        return x * r2 + pltpu.roll(x, n, axis=1) * i2s

    st = lax.dot_general(u, wt, (((1,), (1,)), ((), ())), preferred_element_type=F32)
    cidx = lax.broadcasted_iota(jnp.int32, (rows, 2 * n), 0) % n_chunks
    mag_c = jnp.exp(ar2 * dt2 * tc)
    ang_c = ai2 * dt2 * tc
    r2 = mag_c * jnp.cos(ang_c)
    i2 = sgn * mag_c * jnp.sin(ang_c)
    k = 1
    while k < n_chunks:
        sh = jnp.where(cidx >= k, pltpu.roll(st, k, axis=0), 0.0)
        st = st + cmul(sh, r2, i2)
        r2, i2 = r2 * r2 - i2 * i2, 2.0 * r2 * i2
        k *= 2
    hin = jnp.where(cidx >= 1, pltpu.roll(st, 1, axis=0), 0.0)
    hp = cmul(hin, lr2, sgn * li2)

    y = jnp.dot(u, m_ref[...], preferred_element_type=F32)
    y = y + jnp.dot(hp.astype(BF16), vmat.astype(BF16), preferred_element_type=F32)
    y = y + u.astype(F32) * d_ref[...]
    g = 0.5 * y * (1.0 + lax.erf(y * (2.0 ** -0.5)))
    o_ref[...] = g.astype(o_ref.dtype)


def _s5_gelu(u, a_re, a_im, log_dt, b_re, b_im, c_re, c_im, d_skip):
    bsz, seq, width = u.shape
    n_groups, n = a_re.shape
    p = b_re.shape[-1]
    tc = S5_CHUNK
    n_chunks = seq // tc
    rows = bsz * n_chunks
    tcp = tc * p
    uf = u.reshape(bsz, n_chunks, tc, n_groups, p)
    uf = jnp.transpose(uf, (3, 0, 1, 2, 4)).reshape(n_groups, rows, tcp)
    dt = jnp.exp(log_dt.astype(F32))
    dtb = jnp.broadcast_to(dt[:, None], (n_groups, n))
    acol = jnp.stack([a_re.astype(F32), a_im.astype(F32), dtb], axis=-1)
    arow = jnp.stack([jnp.tile(a_re.astype(F32), (1, 2)), jnp.tile(a_im.astype(F32), (1, 2)),
                      jnp.tile(dtb, (1, 2))], axis=1)
    spb = V7X_LANES // p
    bt = jnp.stack([jnp.tile(b_re.astype(F32), (1, 1, spb)),
                    jnp.tile(b_im.astype(F32), (1, 1, spb))], axis=1)
    btr = jnp.swapaxes(b_re.astype(F32), 1, 2)
    bti = jnp.swapaxes(b_im.astype(F32), 1, 2)
    bx = jnp.stack([jnp.concatenate([btr, bti], axis=-1),
                    jnp.concatenate([bti, btr], axis=-1)], axis=1)
    ctr = jnp.swapaxes(c_re.astype(F32), 1, 2)
    cti = jnp.swapaxes(c_im.astype(F32), 1, 2)
    ct = jnp.stack([jnp.tile(ctr, (1, 1, spb)), jnp.tile(cti, (1, 1, spb))], axis=1)
    dtile = jnp.tile(d_skip.astype(F32).reshape(n_groups, 1, p), (1, 1, tc))

    out = pl.pallas_call(
        functools.partial(_s5_kernel, tc=tc, p=p, n=n, n_chunks=n_chunks),
        grid=(n_groups,),
        in_specs=[
            pl.BlockSpec((None, rows, tcp), lambda g: (g, 0, 0)),
            pl.BlockSpec((None, n, 3), lambda g: (g, 0, 0)),
            pl.BlockSpec((None, 3, 2 * n), lambda g: (g, 0, 0)),
            pl.BlockSpec((None, 2, n, V7X_LANES), lambda g: (g, 0, 0, 0)),
            pl.BlockSpec((None, 2, p, 2 * n), lambda g: (g, 0, 0, 0)),
            pl.BlockSpec((None, 2, n, V7X_LANES), lambda g: (g, 0, 0, 0)),
            pl.BlockSpec((None, 1, tcp), lambda g: (g, 0, 0)),
        ],
        out_specs=pl.BlockSpec((None, rows, tcp), lambda g: (g, 0, 0)),
        out_shape=jax.ShapeDtypeStruct((n_groups, rows, tcp), BF16),
        scratch_shapes=[pltpu.VMEM((tcp, tcp), BF16)],
        compiler_params=_params(("parallel",)),
        name="s5_gelu",
    )(uf, acol, arow, bt, bx, ct, dtile)
    out = out.reshape(n_groups, bsz, n_chunks, tc, p)
    return jnp.transpose(out, (1, 2, 3, 0, 4)).reshape(bsz, seq, width)


def _router_kernel(x_ref, w_ref, idx_ref, gate_ref, *, n_experts):
    logits = jnp.dot(x_ref[...], w_ref[...], preferred_element_type=F32,
                     precision=lax.Precision.HIGHEST)
    col = lax.broadcasted_iota(jnp.int32, logits.shape, 1)
    big = jnp.int32(logits.shape[1])
    lg = jnp.where(col < n_experts, logits, -jnp.inf)
    m1 = jnp.max(lg, axis=-1, keepdims=True)
    i1 = jnp.min(jnp.where(lg == m1, col, big), axis=-1, keepdims=True)
    lg2 = jnp.where(col == i1, -jnp.inf, lg)
    m2 = jnp.max(lg2, axis=-1, keepdims=True)
    i2 = jnp.min(jnp.where(lg2 == m2, col, big), axis=-1, keepdims=True)
    e = jnp.exp(m2 - m1)
    w1 = 1.0 / (1.0 + e)
    w2 = e / (1.0 + e)
    idx_ref[...] = jnp.where(col == 0, i1, jnp.where(col == 1, i2, 0))
    gate_ref[...] = jnp.where(col == 0, w1, jnp.where(col == 1, w2, 0.0))


def _router(x, w_router, *, tm):
    t, dm = x.shape
    n_experts = w_router.shape[1]
    wp = jnp.zeros((dm, V7X_LANES), F32).at[:, :n_experts].set(w_router.astype(F32))
    tm = _pick(t, tm, V7X_SUBLANES)
    idx, gate = pl.pallas_call(
        functools.partial(_router_kernel, n_experts=n_experts),
        grid=(t // tm,),
        in_specs=[pl.BlockSpec((tm, dm), lambda i: (i, 0)),
                  pl.BlockSpec((dm, V7X_LANES), lambda i: (0, 0))],
        out_specs=[pl.BlockSpec((tm, V7X_LANES), lambda i: (i, 0)),
                   pl.BlockSpec((tm, V7X_LANES), lambda i: (i, 0))],
        out_shape=[jax.ShapeDtypeStruct((t, V7X_LANES), jnp.int32),
                   jax.ShapeDtypeStruct((t, V7X_LANES), F32)],
        compiler_params=_params(("parallel",)),
        name="moe_router",
    )(x, wp)
    return idx[:, :TOP_K], gate[:, :TOP_K]


GATHER_UNROLL = 8


def _moe_gate_up_kernel(te_ref, tv_ref, tok_ref, x_hbm, wg_ref, wu_ref, o_ref, xbuf, xb, sem, *,
                        tm):
    i = pl.program_id(0)
    j = pl.program_id(1)
    n_tiles = pl.num_programs(0)
    slot = i % 2
    valid = tv_ref[i] != 0

    def issue_tile(tile, dst_slot):
        def issue(r, carry):
            pltpu.make_async_copy(x_hbm.at[pl.ds(tok_ref[tile * tm + r], 1), :],
                                  xbuf.at[dst_slot, pl.ds(r, 1), :], sem.at[dst_slot]).start()
            return carry
        lax.fori_loop(0, tm, issue, 0, unroll=GATHER_UNROLL)

    def wait_tile(dst_slot):
        pltpu.make_async_copy(x_hbm.at[pl.ds(0, tm), :], xbuf.at[dst_slot],
                              sem.at[dst_slot]).wait()

    @pl.when(jnp.logical_and(j == 0, i == 0))
    def _():
        issue_tile(0, 0)

    @pl.when(jnp.logical_and(j == 0, valid))
    def _():
        wait_tile(slot)

    nxt = jnp.minimum(i + 1, n_tiles - 1)

    @pl.when(jnp.logical_and(j == 0, jnp.logical_and(i + 1 < n_tiles, tv_ref[nxt] != 0)))
    def _():
        issue_tile(i + 1, 1 - slot)

    @pl.when(jnp.logical_and(j == 0, valid))
    def _():
        xb[...] = xbuf[slot].astype(BF16)

    @pl.when(valid)
    def _():
        a = xb[...]
        gate = jnp.dot(a, wg_ref[...], preferred_element_type=F32)
        up = jnp.dot(a, wu_ref[...], preferred_element_type=F32)
        o_ref[...] = _epi_swiglu([gate, up], [], []).astype(o_ref.dtype)

    @pl.when(jnp.logical_not(valid))
    def _():
        o_ref[...] = jnp.zeros_like(o_ref)


def _moe_gate_up(x, w_gate, w_up, row_token, tile_expert, tile_valid, *, tm, tn):
    k = x.shape[1]
    n_rows = row_token.shape[0]
    n = w_gate.shape[2]
    tn = _pick(n, tn, V7X_LANES)
    return pl.pallas_call(
        functools.partial(_moe_gate_up_kernel, tm=tm),
        grid_spec=pltpu.PrefetchScalarGridSpec(
            num_scalar_prefetch=3,
            grid=(n_rows // tm, n // tn),
            in_specs=[pl.BlockSpec(memory_space=pl.ANY),
                      pl.BlockSpec((None, k, tn), lambda i, j, te, tv, tok: (te[i], 0, j)),
                      pl.BlockSpec((None, k, tn), lambda i, j, te, tv, tok: (te[i], 0, j))],
            out_specs=pl.BlockSpec((tm, tn), lambda i, j, te, tv, tok: (i, j)),
            scratch_shapes=[pltpu.VMEM((2, tm, k), x.dtype), pltpu.VMEM((tm, k), BF16),
                            pltpu.SemaphoreType.DMA((2,))],
        ),
        out_shape=jax.ShapeDtypeStruct((n_rows, n), BF16),
        compiler_params=_params(("arbitrary", "arbitrary")),
        name="moe_gate_up",
    )(tile_expert, tile_valid, row_token, x, w_gate, w_up)


def _grouped_kernel(te_ref, tv_ref, *refs, n_rhs, n_row, epilogue):
    lhs_ref = refs[0]
    rhs_refs = refs[1:1 + n_rhs]
    row_refs = refs[1 + n_rhs:1 + n_rhs + n_row]
    o_ref = refs[-1]

    valid = tv_ref[pl.program_id(0)] != 0

    @pl.when(valid)
    def _():
        a = lhs_ref[...]
        accs = [jnp.dot(a, r[...], preferred_element_type=F32) for r in rhs_refs]
        o_ref[...] = epilogue(accs, [r[...] for r in row_refs], []).astype(o_ref.dtype)

    @pl.when(jnp.logical_not(valid))
    def _():
        o_ref[...] = jnp.zeros_like(o_ref)


def _grouped_matmul(lhs, rhss, tile_expert, tile_valid, epilogue, out_dtype, *, tm, tn,
                    row_extras=(), name):
    r, k = lhs.shape
    n = rhss[0].shape[2]
    tn = _pick(n, tn, V7X_LANES)
    in_specs = [pl.BlockSpec((tm, k), lambda i, j, te, tv: (i, 0))]
    in_specs += [pl.BlockSpec((None, k, tn), lambda i, j, te, tv: (te[i], 0, j)) for _ in rhss]
    in_specs += [pl.BlockSpec((tm, 1), lambda i, j, te, tv: (i, 0)) for _ in row_extras]
    return pl.pallas_call(
        functools.partial(_grouped_kernel, n_rhs=len(rhss), n_row=len(row_extras),
                          epilogue=epilogue),
        grid_spec=pltpu.PrefetchScalarGridSpec(
            num_scalar_prefetch=2,
            grid=(r // tm, n // tn),
            in_specs=in_specs,
            out_specs=pl.BlockSpec((tm, tn), lambda i, j, te, tv: (i, j)),
        ),
        out_shape=jax.ShapeDtypeStruct((r, n), out_dtype),
        compiler_params=_params(("parallel", "arbitrary")),
        name=name,
    )(tile_expert, tile_valid, lhs, *rhss, *row_extras)


def _epi_row_scale(accs, rows, cols):
    return accs[0] * rows[0]


def _combine_ln_kernel(pos_ref, y_hbm, x_ref, g_ref, b_ref, of_ref, ob_ref, buf, sem, *,
                       tc, n_tokens, alpha):
    i = pl.program_id(0)
    slot = i % 2

    def issue_tile(tile, dst_slot):
        base = tile * tc

        def issue(r, carry):
            for kk in range(TOP_K):
                pltpu.make_async_copy(y_hbm.at[pl.ds(pos_ref[kk * n_tokens + base + r], 1), :],
                                      buf.at[dst_slot, kk, pl.ds(r, 1), :],
                                      sem.at[dst_slot]).start()
            return carry
        lax.fori_loop(0, tc, issue, 0, unroll=GATHER_UNROLL)

    @pl.when(i == 0)
    def _():
        issue_tile(0, 0)

    for kk in range(TOP_K):
        pltpu.make_async_copy(y_hbm.at[pl.ds(0, tc), :], buf.at[slot, kk], sem.at[slot]).wait()

    @pl.when(i + 1 < pl.num_programs(0))
    def _():
        issue_tile(i + 1, 1 - slot)

    f = buf[slot, 0]
    for kk in range(1, TOP_K):
        f = f + buf[slot, kk]
    y = _layer_norm_rows(alpha * x_ref[...] + f, g_ref[...], b_ref[...])
    of_ref[...] = y
    ob_ref[...] = y.astype(BF16)


def _combine_ln(y_sorted, pos, x, g, b, *, alpha, tc):
    t, dm = x.shape
    tc = _pick(t, tc, V7X_SUBLANES * 2)
    return pl.pallas_call(
        functools.partial(_combine_ln_kernel, tc=tc, n_tokens=t, alpha=alpha),
        grid_spec=pltpu.PrefetchScalarGridSpec(
            num_scalar_prefetch=1,
            grid=(t // tc,),
            in_specs=[pl.BlockSpec(memory_space=pl.ANY),
                      pl.BlockSpec((tc, dm), lambda i, pos: (i, 0)),
                      pl.BlockSpec((1, dm), lambda i, pos: (0, 0)),
                      pl.BlockSpec((1, dm), lambda i, pos: (0, 0))],
            out_specs=[pl.BlockSpec((tc, dm), lambda i, pos: (i, 0)),
                       pl.BlockSpec((tc, dm), lambda i, pos: (i, 0))],
            scratch_shapes=[pltpu.VMEM((2, TOP_K, tc, dm), y_sorted.dtype),
                            pltpu.SemaphoreType.DMA((2,))],
        ),
        out_shape=[jax.ShapeDtypeStruct((t, dm), F32), jax.ShapeDtypeStruct((t, dm), BF16)],
        compiler_params=_params(("arbitrary",)),
        name="moe_combine_ln",
    )(pos, y_sorted, x, g.reshape(1, dm), b.reshape(1, dm))


def _routing_tables(top_idx, top_w, n_experts, tm):
    t = top_idx.shape[0]
    n_pairs = TOP_K * t
    n_rows = n_pairs + n_experts * tm
    n_tiles = n_rows // tm
    e_flat = top_idx.T.reshape(n_pairs)
    w_flat = top_w.T.reshape(n_pairs)
    tok_flat = jnp.tile(jnp.arange(t, dtype=jnp.int32), TOP_K)
    onehot = (e_flat[:, None] == jnp.arange(n_experts, dtype=jnp.int32)[None, :]).astype(jnp.int32)
    rank = jnp.sum((jnp.cumsum(onehot, axis=0) - onehot) * onehot, axis=1)
    counts = jnp.sum(onehot, axis=0)
    tiles_per = (counts + tm - 1) // tm
    tile_end = jnp.cumsum(tiles_per)
    row_start = (tile_end - tiles_per) * tm
    pos = (row_start[e_flat] + rank).astype(jnp.int32)
    row_token = jnp.zeros((n_rows,), jnp.int32).at[pos].set(tok_flat)
    row_gate = jnp.zeros((n_rows,), F32).at[pos].set(w_flat)
    tile_ids = jnp.arange(n_tiles, dtype=jnp.int32)
    total = tile_end[-1]
    tile_valid = (tile_ids < total).astype(jnp.int32)
    clipped = jnp.minimum(tile_ids, total - 1)
    tile_expert = jnp.minimum(
        jnp.sum((tile_end[None, :] <= clipped[:, None]).astype(jnp.int32), axis=1),
        n_experts - 1).astype(jnp.int32)
    return pos, row_token, row_gate.reshape(n_rows, 1), tile_expert, tile_valid


def kernel(x, w_in, w_out, attn_lambda_q1, attn_lambda_k1, attn_lambda_q2, attn_lambda_k2, attn_subln_w, ssm_a_re, ssm_a_im, ssm_log_dt, ssm_b_re, ssm_b_im, ssm_c_re, ssm_c_im, ssm_d, ssm_w_glu, ln1_g, ln1_b, ln2_g, ln2_b, ffn_w_gate, ffn_w_up, ffn_w_down, moe_w_router, moe_w_gate, moe_w_up, moe_w_down):
    bsz, seq, dm = x.shape
    depth = w_in.shape[0]
    d = attn_lambda_q1.shape[-1]
    ssm_width = ssm_d.shape[-1]
    attn_width = w_out.shape[1] - ssm_width
    n_heads = attn_width // (2 * d)
    n_experts = moe_w_router.shape[-1]
    t = bsz * seq
    alpha = (2.0 * depth) ** 0.25
    moe_tm = _pick(t, 512, V7X_SUBLANES * 2)

    xf = x.reshape(t, dm).astype(F32)
    xb = xf.astype(BF16)
    col_scale = jnp.concatenate([jnp.full((1, attn_width), d ** -0.5 * math.log2(math.e), F32),
                                 jnp.ones((1, w_in.shape[2] - attn_width), F32)], axis=1)

    for l in range(depth):
        lam_init = 0.8 - 0.6 * math.exp(-0.3 * l)
        h = _matmul(xb, [w_in[l].astype(BF16)], _epi_scale_cols, BF16, tm=1024, tn=512,
                    col_extras=[col_scale], name=f"w_in_{l}")
        h = h.reshape(bsz, seq, -1)
        lam_params = jnp.stack([attn_lambda_q1[l], attn_lambda_k1[l],
                                attn_lambda_q2[l], attn_lambda_k2[l]]).astype(F32)
        o = _diff_attention(h, lam_params, attn_subln_w[l].astype(F32), n_heads=n_heads, d=d,
                            lam_init=lam_init, tq=1024, tk=1024, tk_diag=512)
        g = _s5_gelu(h[:, :, 3 * attn_width:], ssm_a_re[l], ssm_a_im[l], ssm_log_dt[l],
                     ssm_b_re[l], ssm_b_im[l], ssm_c_re[l], ssm_c_im[l], ssm_d[l])
        g = g.reshape(t, ssm_width)
        y = _matmul(g, [ssm_w_glu[l].astype(BF16)], _epi_glu, BF16, tm=1024, tn=512,
                    row_extras=[g], name=f"ssm_glu_{l}")
        xf, xb = _matmul_ln([o.reshape(t, attn_width), y], w_out[l].astype(BF16), xf,
                            ln1_g[l].astype(F32), ln1_b[l].astype(F32), alpha=alpha,
                            tm=512, tk=512, name=f"w_out_ln_{l}")
        if l % 2 == 0:
            e = l // 2
            pad = (-ffn_w_gate.shape[-1]) % FFN_K_TILE
            w_gate = jnp.pad(ffn_w_gate[e].astype(BF16), ((0, 0), (0, pad)))
            w_up = jnp.pad(ffn_w_up[e].astype(BF16), ((0, 0), (0, pad)))
            w_down = jnp.pad(ffn_w_down[e].astype(BF16), ((0, pad), (0, 0)))
            hid = _matmul(xb, [w_gate, w_up], _epi_swiglu, BF16, tm=1024, tn=256,
                          name=f"ffn_gate_up_{l}")
            xf, xb = _matmul_ln([hid], w_down, xf, ln2_g[l].astype(F32),
                                ln2_b[l].astype(F32), alpha=alpha, tm=512, tk=FFN_K_TILE,
                                name=f"ffn_down_ln_{l}")
        else:
            e = l // 2
            top_idx, top_w = _router(xf, moe_w_router[e], tm=512)
            pos, row_token, row_gate, tile_expert, tile_valid = _routing_tables(
                top_idx, top_w, n_experts, moe_tm)
            hid = _moe_gate_up(xf, moe_w_gate[e].astype(BF16), moe_w_up[e].astype(BF16),
                               row_token, tile_expert, tile_valid, tm=moe_tm, tn=512)
            ys = _grouped_matmul(hid, [moe_w_down[e].astype(BF16)], tile_expert, tile_valid,
                                 _epi_row_scale, F32, tm=moe_tm, tn=1024,
                                 row_extras=[row_gate], name=f"moe_down_{l}")
            xf, xb = _combine_ln(ys, pos, xf, ln2_g[l].astype(F32), ln2_b[l].astype(F32),
                                 alpha=alpha, tc=256)
    return xf.reshape(bsz, seq, dm).astype(x.dtype)
```

```python
import functools
import math

import jax
import jax.numpy as jnp
from jax import lax
from jax.experimental import pallas as pl
from jax.experimental.pallas import tpu as pltpu

F32 = jnp.float32
BF16 = jnp.bfloat16

V7X_LANES = 128
V7X_SUBLANES = 8
V7X_VMEM_LIMIT_BYTES = 56 * 1024 * 1024

ATTN_CHUNK = 64
TOP_K = 2
LN_EPS = 1e-5
RMS_EPS = 1e-5
MASK_VALUE = -1e30
S5_CHUNK = 64
FFN_K_TILE = 512


def _pick(dim, target, align):
    if dim <= target:
        return dim
    t = (target // align) * align
    while t >= align:
        if dim % t == 0:
            return t
        t -= align
    raise ValueError(f"no tile for dim={dim} target={target} align={align}")


def _params(semantics):
    return pltpu.CompilerParams(dimension_semantics=semantics,
                                vmem_limit_bytes=V7X_VMEM_LIMIT_BYTES)


def _mm_kernel(*refs, n_rhs, n_row, n_col, epilogue):
    lhs_ref = refs[0]
    rhs_refs = refs[1:1 + n_rhs]
    row_refs = refs[1 + n_rhs:1 + n_rhs + n_row]
    col_refs = refs[1 + n_rhs + n_row:1 + n_rhs + n_row + n_col]
    o_ref = refs[-1]
    a = lhs_ref[...].astype(BF16)
    accs = [jnp.dot(a, r[...], preferred_element_type=F32) for r in rhs_refs]
    out = epilogue(accs, [r[...] for r in row_refs], [c[...] for c in col_refs])
    o_ref[...] = out.astype(o_ref.dtype)


def _matmul(lhs, rhss, epilogue, out_dtype, *, tm, tn, row_extras=(), col_extras=(), name):
    m, k = lhs.shape
    n = rhss[0].shape[1]
    tm = _pick(m, tm, V7X_SUBLANES * 2)
    tn = _pick(n, tn, V7X_LANES)
    in_specs = [pl.BlockSpec((tm, k), lambda i, j: (i, 0))]
    in_specs += [pl.BlockSpec((k, tn), lambda i, j: (0, j)) for _ in rhss]
    in_specs += [pl.BlockSpec((tm, tn), lambda i, j: (i, j)) for _ in row_extras]
    in_specs += [pl.BlockSpec((1, tn), lambda i, j: (0, j)) for _ in col_extras]
    body = functools.partial(_mm_kernel, n_rhs=len(rhss), n_row=len(row_extras),
                             n_col=len(col_extras), epilogue=epilogue)
    return pl.pallas_call(
        body,
        grid=(m // tm, n // tn),
        in_specs=in_specs,
        out_specs=pl.BlockSpec((tm, tn), lambda i, j: (i, j)),
        out_shape=jax.ShapeDtypeStruct((m, n), out_dtype),
        compiler_params=_params(("parallel", "arbitrary")),
        name=name,
    )(lhs, *rhss, *row_extras, *col_extras)


def _epi_scale_cols(accs, rows, cols):
    return accs[0] * cols[0]


def _epi_first(accs, rows, cols):
    return accs[0]


def _epi_swiglu(accs, rows, cols):
    g = accs[0]
    return g * jax.nn.sigmoid(g) * accs[1]


def _epi_glu(accs, rows, cols):
    g = rows[0].astype(F32)
    return g * jax.nn.sigmoid(accs[0])


def _layer_norm_rows(r, g, b):
    mu = jnp.mean(r, axis=-1, keepdims=True)
    c = r - mu
    var = jnp.mean(c * c, axis=-1, keepdims=True)
    return c * lax.rsqrt(var + LN_EPS) * g + b


def _mm_ln_kernel(*refs, alpha, part_blocks):
    n_parts = len(part_blocks)
    lhs_refs = refs[:n_parts]
    rhs_ref, x_ref, g_ref, b_ref, of_ref, ob_ref = refs[n_parts:]
    kk = pl.program_id(1)

    @pl.when(kk == 0)
    def _():
        of_ref[...] = jnp.dot(lhs_refs[0][...], rhs_ref[...], preferred_element_type=F32)

    first = 0
    for lhs_ref, blocks in zip(lhs_refs, part_blocks):
        @pl.when(jnp.logical_and(kk >= max(first, 1), kk < first + blocks))
        def _(lhs_ref=lhs_ref):
            of_ref[...] += jnp.dot(lhs_ref[...], rhs_ref[...], preferred_element_type=F32)
        first += blocks

    @pl.when(kk == pl.num_programs(1) - 1)
    def _():
        y = _layer_norm_rows(alpha * x_ref[...] + of_ref[...], g_ref[...], b_ref[...])
        of_ref[...] = y
        ob_ref[...] = y.astype(BF16)


def _matmul_ln(lhs_parts, rhs, x, g, b, *, alpha, tm, tk, name):
    m = lhs_parts[0].shape[0]
    n = rhs.shape[1]
    tm = _pick(m, tm, V7X_SUBLANES * 2)
    tk = _pick(math.gcd(*[p.shape[1] for p in lhs_parts]), tk, V7X_LANES)
    part_blocks = tuple(p.shape[1] // tk for p in lhs_parts)

    def part_spec(first, blocks):
        return pl.BlockSpec((tm, tk),
                            lambda i, kk: (i, jnp.clip(kk - first, 0, blocks - 1)))

    firsts = [sum(part_blocks[:p]) for p in range(len(part_blocks))]
    return pl.pallas_call(
        functools.partial(_mm_ln_kernel, alpha=alpha, part_blocks=part_blocks),
        grid=(m // tm, sum(part_blocks)),
        in_specs=[part_spec(f, nb) for f, nb in zip(firsts, part_blocks)] + [
            pl.BlockSpec((tk, n), lambda i, kk: (kk, 0)),
            pl.BlockSpec((tm, n), lambda i, kk: (i, 0), pipeline_mode=pl.Buffered(1)),
            pl.BlockSpec((1, n), lambda i, kk: (0, 0)),
            pl.BlockSpec((1, n), lambda i, kk: (0, 0)),
        ],
        out_specs=[pl.BlockSpec((tm, n), lambda i, kk: (i, 0)),
                   pl.BlockSpec((tm, n), lambda i, kk: (i, 0))],
        out_shape=[jax.ShapeDtypeStruct((m, n), F32), jax.ShapeDtypeStruct((m, n), BF16)],
        compiler_params=_params(("parallel", "arbitrary")),
        name=name,
    )(*lhs_parts, rhs, x, g.reshape(1, n), b.reshape(1, n))


def _attn_kernel(lam_ref, w_ref, q_ref, k_ref, v_ref, o_ref, m_sc, l_sc, acc_sc, *,
                 tq, tk, tk_diag, d, lam_init):
    qi = pl.program_id(2)
    hw = 2 * d
    lanes = V7X_LANES
    m_sc[...] = jnp.full(m_sc.shape, MASK_VALUE, F32)
    l_sc[...] = jnp.zeros_like(l_sc)
    acc_sc[...] = jnp.zeros_like(acc_sc)

    def step(start, width, diag_block):
        r0 = 0 if diag_block is None else diag_block * width
        nr = tq - r0
        start = pl.multiple_of(start, width)
        kb = k_ref[pl.ds(start, width), :]
        vb = v_ref[pl.ds(start, width), :]
        if diag_block is not None:
            row = lax.broadcasted_iota(jnp.int32, (nr, width), 0) // ATTN_CHUNK
            col = lax.broadcasted_iota(jnp.int32, (nr, width), 1) // ATTN_CHUNK
            allowed = col <= row
        scores = [lax.dot_general(q_ref[r0:tq, mp * d:(mp + 1) * d], kb[:, mp * d:(mp + 1) * d],
                                  (((1,), (1,)), ((), ())), preferred_element_type=F32)
                  for mp in range(2)]
        for mp in range(2):
            s = scores[mp]
            if diag_block is not None:
                s = jnp.where(allowed, s, MASK_VALUE)
            m_prev = m_sc[mp, r0:tq]
            m_next = jnp.maximum(m_prev, jnp.max(s, axis=-1, keepdims=True))
            alpha = jnp.exp2(m_prev - m_next)
            ps = [jnp.exp2(s[:, c * lanes:(c + 1) * lanes] - m_next) for c in range(width // lanes)]
            psum = ps[0]
            for pc in ps[1:]:
                psum = psum + pc
            l_sc[mp, r0:tq] = alpha * l_sc[mp, r0:tq] + psum
            p = jnp.concatenate(ps, axis=1).astype(BF16)
            alpha_w = jnp.concatenate([alpha] * (hw // lanes), axis=1)
            acc_sc[mp, r0:tq] = alpha_w * acc_sc[mp, r0:tq] + jnp.dot(
                p, vb, preferred_element_type=F32)
            m_sc[mp, r0:tq] = m_next

    def body(j, carry):
        step(j * tk, tk, None)
        return carry

    lax.fori_loop(0, qi * (tq // tk), body, 0)
    for c in range(tq // tk_diag):
        step(qi * tq + c * tk_diag, tk_diag, c)

    lam_p = lam_ref[...]
    lam = (jnp.exp(jnp.sum(lam_p[0:1] * lam_p[1:2], axis=-1, keepdims=True))
           - jnp.exp(jnp.sum(lam_p[2:3] * lam_p[3:4], axis=-1, keepdims=True)) + lam_init)
    l0 = jnp.sum(l_sc[0], axis=-1, keepdims=True)
    l1 = jnp.sum(l_sc[1], axis=-1, keepdims=True)
    o = acc_sc[0] / l0 - lam * (acc_sc[1] / l1)
    o = o * lax.rsqrt(jnp.mean(o * o, axis=-1, keepdims=True) + RMS_EPS)
    o_ref[...] = (o * w_ref[...] * (1.0 - lam_init)).astype(o_ref.dtype)


def _diff_attention(h, lam_params, subln_w, *, n_heads, d, lam_init, tq, tk, tk_diag):
    bsz, seq, _ = h.shape
    hw = 2 * d
    tq = _pick(seq, tq, V7X_LANES)
    tk = _pick(tq, tk, V7X_LANES)
    tk_diag = _pick(tq, tk_diag, V7X_LANES)
    return pl.pallas_call(
        functools.partial(_attn_kernel, tq=tq, tk=tk, tk_diag=tk_diag, d=d, lam_init=lam_init),
        grid=(bsz, n_heads, seq // tq),
        in_specs=[
            pl.BlockSpec((4, d), lambda b, hd, i: (0, 0)),
            pl.BlockSpec((1, hw), lambda b, hd, i: (0, 0)),
            pl.BlockSpec((None, tq, hw), lambda b, hd, i: (b, i, hd)),
            pl.BlockSpec((None, seq, hw), lambda b, hd, i: (b, 0, n_heads + hd)),
            pl.BlockSpec((None, seq, hw), lambda b, hd, i: (b, 0, 2 * n_heads + hd)),
        ],
        out_specs=pl.BlockSpec((None, tq, hw), lambda b, hd, i: (b, i, hd)),
        out_shape=jax.ShapeDtypeStruct((bsz, seq, n_heads * hw), BF16),
        scratch_shapes=[pltpu.VMEM((2, tq, V7X_LANES), F32), pltpu.VMEM((2, tq, V7X_LANES), F32),
                        pltpu.VMEM((2, tq, hw), F32)],
        compiler_params=_params(("parallel", "parallel", "arbitrary")),
        name="diff_attention",
    )(lam_params, subln_w.reshape(1, hw), h, h, h)


def _s5_kernel(u_ref, acol_ref, arow_ref, bt_ref, bx_ref, ct_ref, d_ref, perm_ref, permt_ref,
               o_ref, m_ref, u_all, y_all, *, tc, p, n, n_chunks, rows):
    tcp = tc * p
    lanes = V7X_LANES
    spb = lanes // p
    gpb = lanes // p
    n_blk = tcp // lanes
    gi = pl.program_id(1)

    @pl.when(gi == 0)
    def _():
        for k in range(n_blk):
            zk = jnp.concatenate(
                [u_ref[pl.ds(k * spb + s, rows, stride=tc), :] for s in range(spb)], axis=1)
            ok = jnp.dot(zk.astype(BF16), perm_ref[...], preferred_element_type=F32)
            for g2 in range(gpb):
                u_all[g2, :, k * lanes:(k + 1) * lanes] = (
                    ok[:, g2 * lanes:(g2 + 1) * lanes].astype(BF16))

    u = u_all[gi]

    ar = acol_ref[:, 0:1]
    ai = acol_ref[:, 1:2]
    dt = acol_ref[:, 2:3]
    lane_id = lax.broadcasted_iota(jnp.int32, (1, lanes), 1)
    tau0 = (lane_id // p).astype(F32)

    def powers(t):
        mag = jnp.exp((ar * dt) * t)
        ang = (ai * dt) * t
        return mag * jnp.cos(ang), mag * jnp.sin(ang)

    def cmul2(a_re, a_im, b_re, b_im):
        return a_re * b_re - a_im * b_im, a_re * b_im + a_im * b_re

    mag1 = jnp.exp(ar * dt)
    lr = mag1 * jnp.cos(ai * dt)
    li = mag1 * jnp.sin(ai * dt)
    den = ar * ar + ai * ai
    f_re = ((lr - 1.0) * ar + li * ai) / den
    f_im = (li * ar - (lr - 1.0) * ai) / den
    bb_re, bb_im = cmul2(f_re, f_im, bt_ref[0], bt_ref[1])

    blk_re, blk_im = powers((lane_id * spb).astype(F32))
    zb_re, zb_im = cmul2(*powers((spb - 1.0) - tau0), bb_re, bb_im)
    vb_re, vb_im = cmul2(*powers(tau0), ct_ref[0], ct_ref[1])
    z_blocks, v_blocks = [], []
    for k in range(n_blk):
        kr = n_blk - 1 - k
        z_blocks.append(cmul2(zb_re, zb_im, blk_re[:, kr:kr + 1], blk_im[:, kr:kr + 1]))
        v_blocks.append(cmul2(vb_re, vb_im, blk_re[:, k:k + 1], blk_im[:, k:k + 1]))
    z_re = jnp.concatenate([z[0] for z in z_blocks], axis=1)
    z_im = jnp.concatenate([z[1] for z in z_blocks], axis=1)
    wt = jnp.concatenate([z_re, z_im], axis=0).astype(BF16)
    v_re = jnp.concatenate([v[0] for v in v_blocks], axis=1)
    v_im = jnp.concatenate([v[1] for v in v_blocks], axis=1)
    vmat = jnp.concatenate([v_re, -v_im], axis=0)

    ar2 = arow_ref[0:1, :]
    ai2 = arow_ref[1:2, :]
    dt2 = arow_ref[2:3, :]
    half = lax.broadcasted_iota(jnp.int32, (1, 2 * n), 1) < n
    sgn = jnp.where(half, -1.0, 1.0).astype(F32)
    mag1r = jnp.exp(ar2 * dt2)
    lr2 = mag1r * jnp.cos(ai2 * dt2)
    li2 = mag1r * jnp.sin(ai2 * dt2)
    den2 = ar2 * ar2 + ai2 * ai2
    fr2 = ((lr2 - 1.0) * ar2 + li2 * ai2) / den2
    fi2 = (li2 * ar2 - (lr2 - 1.0) * ai2) / den2
    bbt = fr2 * bx_ref[0] + (sgn * fi2) * bx_ref[1]
    kq = jnp.dot(bbt, vmat, preferred_element_type=F32, precision=lax.Precision.HIGHEST)

    lane = lax.broadcasted_iota(jnp.int32, (p, tcp), 1)
    m_ref[0:p, :] = kq.astype(BF16)
    for s in range(1, tc):
        blk = jnp.where(lane >= s * p, pltpu.roll(kq, s * p, axis=1), 0.0)
        m_ref[s * p:(s + 1) * p, :] = blk.astype(BF16)

    def cmul(x, r2, i2s):
        return x * r2 + pltpu.roll(x, n, axis=1) * i2s

    st = lax.dot_general(u, wt, (((1,), (1,)), ((), ())), preferred_element_type=F32)
    cidx = lax.broadcasted_iota(jnp.int32, (rows, 2 * n), 0) % n_chunks
    mag_c = jnp.exp(ar2 * dt2 * tc)
    ang_c = ai2 * dt2 * tc
    r2 = mag_c * jnp.cos(ang_c)
    i2 = sgn * mag_c * jnp.sin(ang_c)
    k = 1
    while k < n_chunks:
        sh = jnp.where(cidx >= k, pltpu.roll(st, k, axis=0), 0.0)
        st = st + cmul(sh, r2, i2)
        r2, i2 = r2 * r2 - i2 * i2, 2.0 * r2 * i2
        k *= 2
    hin = jnp.where(cidx >= 1, pltpu.roll(st, 1, axis=0), 0.0)
    hp = cmul(hin, lr2, sgn * li2)

    y = jnp.dot(u, m_ref[...], preferred_element_type=F32)
    y = y + jnp.dot(hp.astype(BF16), vmat.astype(BF16), preferred_element_type=F32)
    y = y + u.astype(F32) * d_ref[...]
    g = 0.5 * y * (1.0 + lax.erf(y * (2.0 ** -0.5)))
    y_all[gi] = g.astype(BF16)

    @pl.when(gi == gpb - 1)
    def _():
        for k in range(n_blk):
            wk = jnp.concatenate(
                [y_all[g2, :, k * lanes:(k + 1) * lanes] for g2 in range(gpb)], axis=1)
            ok = jnp.dot(wk, permt_ref[...], preferred_element_type=F32)
            for s in range(spb):
                o_ref[pl.ds(k * spb + s, rows, stride=tc), :] = ok[:, s * lanes:(s + 1) * lanes]


def _s5_gelu(u, a_re, a_im, log_dt, b_re, b_im, c_re, c_im, d_skip, *, bsz):
    t, width = u.shape
    seq = t // bsz
    n_groups, n = a_re.shape
    p = b_re.shape[-1]
    tc = S5_CHUNK
    n_chunks = seq // tc
    rows = bsz * n_chunks
    tcp = tc * p
    lanes = V7X_LANES
    gpb = lanes // p
    dt = jnp.exp(log_dt.astype(F32))
    dtb = jnp.broadcast_to(dt[:, None], (n_groups, n))
    acol = jnp.stack([a_re.astype(F32), a_im.astype(F32), dtb], axis=-1)
    arow = jnp.stack([jnp.tile(a_re.astype(F32), (1, 2)), jnp.tile(a_im.astype(F32), (1, 2)),
                      jnp.tile(dtb, (1, 2))], axis=1)
    spb = lanes // p
    bt = jnp.stack([jnp.tile(b_re.astype(F32), (1, 1, spb)),
                    jnp.tile(b_im.astype(F32), (1, 1, spb))], axis=1)
    btr = jnp.swapaxes(b_re.astype(F32), 1, 2)
    bti = jnp.swapaxes(b_im.astype(F32), 1, 2)
    bx = jnp.stack([jnp.concatenate([btr, bti], axis=-1),
                    jnp.concatenate([bti, btr], axis=-1)], axis=1)
    ctr = jnp.swapaxes(c_re.astype(F32), 1, 2)
    cti = jnp.swapaxes(c_im.astype(F32), 1, 2)
    ct = jnp.stack([jnp.tile(ctr, (1, 1, spb)), jnp.tile(cti, (1, 1, spb))], axis=1)
    dtile = jnp.tile(d_skip.astype(F32).reshape(n_groups, 1, p), (1, 1, tc))

    src = jnp.arange(spb * lanes, dtype=jnp.int32)
    dst = ((src % lanes) // p) * (spb * p) + (src // lanes) * p + src % p
    perm = (dst[:, None] == src[None, :]).astype(BF16)
    permt = perm.T

    def grp(*tail):
        return lambda cb, gi: (cb * gpb + gi,) + tail

    return pl.pallas_call(
        functools.partial(_s5_kernel, tc=tc, p=p, n=n, n_chunks=n_chunks, rows=rows),
        grid=(n_groups // gpb, gpb),
        in_specs=[
            pl.BlockSpec((t, lanes), lambda cb, gi: (0, cb)),
            pl.BlockSpec((None, n, 3), grp(0, 0)),
            pl.BlockSpec((None, 3, 2 * n), grp(0, 0)),
            pl.BlockSpec((None, 2, n, lanes), grp(0, 0, 0)),
            pl.BlockSpec((None, 2, p, 2 * n), grp(0, 0, 0)),
            pl.BlockSpec((None, 2, n, lanes), grp(0, 0, 0)),
            pl.BlockSpec((None, 1, tcp), grp(0, 0)),
            pl.BlockSpec((spb * lanes, spb * lanes), lambda cb, gi: (0, 0)),
            pl.BlockSpec((spb * lanes, spb * lanes), lambda cb, gi: (0, 0)),
        ],
        out_specs=pl.BlockSpec((t, lanes), lambda cb, gi: (0, cb)),
        out_shape=jax.ShapeDtypeStruct((t, width), F32),
        scratch_shapes=[pltpu.VMEM((tcp, tcp), BF16), pltpu.VMEM((gpb, rows, tcp), BF16),
                        pltpu.VMEM((gpb, rows, tcp), BF16)],
        compiler_params=_params(("parallel", "arbitrary")),
        name="s5_gelu",
    )(u, acol, arow, bt, bx, ct, dtile, perm, permt)


def _router_kernel(x_ref, w_ref, idx_ref, gate_ref, *, n_experts):
    logits = jnp.dot(x_ref[...], w_ref[...], preferred_element_type=F32,
                     precision=lax.Precision.HIGHEST)
    col = lax.broadcasted_iota(jnp.int32, logits.shape, 1)
    big = jnp.int32(logits.shape[1])
    lg = jnp.where(col < n_experts, logits, -jnp.inf)
    m1 = jnp.max(lg, axis=-1, keepdims=True)
    i1 = jnp.min(jnp.where(lg == m1, col, big), axis=-1, keepdims=True)
    lg2 = jnp.where(col == i1, -jnp.inf, lg)
    m2 = jnp.max(lg2, axis=-1, keepdims=True)
    i2 = jnp.min(jnp.where(lg2 == m2, col, big), axis=-1, keepdims=True)
    e = jnp.exp(m2 - m1)
    w1 = 1.0 / (1.0 + e)
    w2 = e / (1.0 + e)
    idx_ref[...] = jnp.where(col == 0, i1, jnp.where(col == 1, i2, 0))
    gate_ref[...] = jnp.where(col == 0, w1, jnp.where(col == 1, w2, 0.0))


def _router(x, w_router, *, tm):
    t, dm = x.shape
    n_experts = w_router.shape[1]
    wp = jnp.zeros((dm, V7X_LANES), F32).at[:, :n_experts].set(w_router.astype(F32))
    tm = _pick(t, tm, V7X_SUBLANES)
    idx, gate = pl.pallas_call(
        functools.partial(_router_kernel, n_experts=n_experts),
        grid=(t // tm,),
        in_specs=[pl.BlockSpec((tm, dm), lambda i: (i, 0)),
                  pl.BlockSpec((dm, V7X_LANES), lambda i: (0, 0))],
        out_specs=[pl.BlockSpec((tm, V7X_LANES), lambda i: (i, 0)),
                   pl.BlockSpec((tm, V7X_LANES), lambda i: (i, 0))],
        out_shape=[jax.ShapeDtypeStruct((t, V7X_LANES), jnp.int32),
                   jax.ShapeDtypeStruct((t, V7X_LANES), F32)],
        compiler_params=_params(("parallel",)),
        name="moe_router",
    )(x, wp)
    return idx[:, :TOP_K], gate[:, :TOP_K]


GATHER_UNROLL = 8


def _moe_gate_up_kernel(te_ref, tv_ref, tok_ref, x_hbm, wg_ref, wu_ref, o_ref, xbuf, xb, sem, *,
                        tm):
    i = pl.program_id(0)
    j = pl.program_id(1)
    n_tiles = pl.num_programs(0)
    slot = i % 2
    valid = tv_ref[i] != 0

    def issue_tile(tile, dst_slot):
        def issue(r, carry):
            pltpu.make_async_copy(x_hbm.at[pl.ds(tok_ref[tile * tm + r], 1), :],
                                  xbuf.at[dst_slot, pl.ds(r, 1), :], sem.at[dst_slot]).start()
            return carry
        lax.fori_loop(0, tm, issue, 0, unroll=GATHER_UNROLL)

    def wait_tile(dst_slot):
        pltpu.make_async_copy(x_hbm.at[pl.ds(0, tm), :], xbuf.at[dst_slot],
                              sem.at[dst_slot]).wait()

    @pl.when(jnp.logical_and(j == 0, i == 0))
    def _():
        issue_tile(0, 0)

    @pl.when(jnp.logical_and(j == 0, valid))
    def _():
        wait_tile(slot)

    nxt = jnp.minimum(i + 1, n_tiles - 1)

    @pl.when(jnp.logical_and(j == 0, jnp.logical_and(i + 1 < n_tiles, tv_ref[nxt] != 0)))
    def _():
        issue_tile(i + 1, 1 - slot)

    @pl.when(jnp.logical_and(j == 0, valid))
    def _():
        xb[...] = xbuf[slot].astype(BF16)

    @pl.when(valid)
    def _():
        a = xb[...]
        gate = jnp.dot(a, wg_ref[...], preferred_element_type=F32)
        up = jnp.dot(a, wu_ref[...], preferred_element_type=F32)
        o_ref[...] = _epi_swiglu([gate, up], [], []).astype(o_ref.dtype)

    @pl.when(jnp.logical_not(valid))
    def _():
        o_ref[...] = jnp.zeros_like(o_ref)


def _moe_gate_up(x, w_gate, w_up, row_token, tile_expert, tile_valid, *, tm, tn):
    k = x.shape[1]
    n_rows = row_token.shape[0]
    n = w_gate.shape[2]
    tn = _pick(n, tn, V7X_LANES)
    return pl.pallas_call(
        functools.partial(_moe_gate_up_kernel, tm=tm),
        grid_spec=pltpu.PrefetchScalarGridSpec(
            num_scalar_prefetch=3,
            grid=(n_rows // tm, n // tn),
            in_specs=[pl.BlockSpec(memory_space=pl.ANY),
                      pl.BlockSpec((None, k, tn), lambda i, j, te, tv, tok: (te[i], 0, j)),
                      pl.BlockSpec((None, k, tn), lambda i, j, te, tv, tok: (te[i], 0, j))],
            out_specs=pl.BlockSpec((tm, tn), lambda i, j, te, tv, tok: (i, j)),
            scratch_shapes=[pltpu.VMEM((2, tm, k), x.dtype), pltpu.VMEM((tm, k), BF16),
                            pltpu.SemaphoreType.DMA((2,))],
        ),
        out_shape=jax.ShapeDtypeStruct((n_rows, n), BF16),
        compiler_params=_params(("arbitrary", "arbitrary")),
        name="moe_gate_up",
    )(tile_expert, tile_valid, row_token, x, w_gate, w_up)


def _grouped_kernel(te_ref, tv_ref, *refs, n_rhs, n_row, epilogue):
    lhs_ref = refs[0]
    rhs_refs = refs[1:1 + n_rhs]
    row_refs = refs[1 + n_rhs:1 + n_rhs + n_row]
    o_ref = refs[-1]

    valid = tv_ref[pl.program_id(0)] != 0

    @pl.when(valid)
    def _():
        a = lhs_ref[...]
        accs = [jnp.dot(a, r[...], preferred_element_type=F32) for r in rhs_refs]
        o_ref[...] = epilogue(accs, [r[...] for r in row_refs], []).astype(o_ref.dtype)

    @pl.when(jnp.logical_not(valid))
    def _():
        o_ref[...] = jnp.zeros_like(o_ref)


def _grouped_matmul(lhs, rhss, tile_expert, tile_valid, epilogue, out_dtype, *, tm, tn,
                    row_extras=(), name):
    r, k = lhs.shape
    n = rhss[0].shape[2]
    tn = _pick(n, tn, V7X_LANES)
    in_specs = [pl.BlockSpec((tm, k), lambda i, j, te, tv: (i, 0))]
    in_specs += [pl.BlockSpec((None, k, tn), lambda i, j, te, tv: (te[i], 0, j)) for _ in rhss]
    in_specs += [pl.BlockSpec((tm, 1), lambda i, j, te, tv: (i, 0)) for _ in row_extras]
    return pl.pallas_call(
        functools.partial(_grouped_kernel, n_rhs=len(rhss), n_row=len(row_extras),
                          epilogue=epilogue),
        grid_spec=pltpu.PrefetchScalarGridSpec(
            num_scalar_prefetch=2,
            grid=(r // tm, n // tn),
            in_specs=in_specs,
            out_specs=pl.BlockSpec((tm, tn), lambda i, j, te, tv: (i, j)),
        ),
        out_shape=jax.ShapeDtypeStruct((r, n), out_dtype),
        compiler_params=_params(("parallel", "arbitrary")),
        name=name,
    )(tile_expert, tile_valid, lhs, *rhss, *row_extras)


def _epi_row_scale(accs, rows, cols):
    return accs[0] * rows[0]


def _combine_ln_kernel(pos_ref, y_hbm, x_ref, g_ref, b_ref, of_ref, ob_ref, buf, sem, *,
                       tc, n_tokens, alpha):
    i = pl.program_id(0)
    slot = i % 2

    def issue_tile(tile, dst_slot):
        base = tile * tc

        def issue(r, carry):
            for kk in range(TOP_K):
                pltpu.make_async_copy(y_hbm.at[pl.ds(pos_ref[kk * n_tokens + base + r], 1), :],
                                      buf.at[dst_slot, kk, pl.ds(r, 1), :],
                                      sem.at[dst_slot]).start()
            return carry
        lax.fori_loop(0, tc, issue, 0, unroll=GATHER_UNROLL)

    @pl.when(i == 0)
    def _():
        issue_tile(0, 0)

    for kk in range(TOP_K):
        pltpu.make_async_copy(y_hbm.at[pl.ds(0, tc), :], buf.at[slot, kk], sem.at[slot]).wait()

    @pl.when(i + 1 < pl.num_programs(0))
    def _():
        issue_tile(i + 1, 1 - slot)

    f = buf[slot, 0]
    for kk in range(1, TOP_K):
        f = f + buf[slot, kk]
    y = _layer_norm_rows(alpha * x_ref[...] + f, g_ref[...], b_ref[...])
    of_ref[...] = y
    ob_ref[...] = y.astype(BF16)


def _combine_ln(y_sorted, pos, x, g, b, *, alpha, tc):
    t, dm = x.shape
    tc = _pick(t, tc, V7X_SUBLANES * 2)
    return pl.pallas_call(
        functools.partial(_combine_ln_kernel, tc=tc, n_tokens=t, alpha=alpha),
        grid_spec=pltpu.PrefetchScalarGridSpec(
            num_scalar_prefetch=1,
            grid=(t // tc,),
            in_specs=[pl.BlockSpec(memory_space=pl.ANY),
                      pl.BlockSpec((tc, dm), lambda i, pos: (i, 0)),
                      pl.BlockSpec((1, dm), lambda i, pos: (0, 0)),
                      pl.BlockSpec((1, dm), lambda i, pos: (0, 0))],
            out_specs=[pl.BlockSpec((tc, dm), lambda i, pos: (i, 0)),
                       pl.BlockSpec((tc, dm), lambda i, pos: (i, 0))],
            scratch_shapes=[pltpu.VMEM((2, TOP_K, tc, dm), y_sorted.dtype),
                            pltpu.SemaphoreType.DMA((2,))],
        ),
        out_shape=[jax.ShapeDtypeStruct((t, dm), F32), jax.ShapeDtypeStruct((t, dm), BF16)],
        compiler_params=_params(("arbitrary",)),
        name="moe_combine_ln",
    )(pos, y_sorted, x, g.reshape(1, dm), b.reshape(1, dm))


def _routing_tables(top_idx, top_w, n_experts, tm):
    t = top_idx.shape[0]
    n_pairs = TOP_K * t
    n_rows = n_pairs + n_experts * tm
    n_tiles = n_rows // tm
    e_flat = top_idx.T.reshape(n_pairs)
    w_flat = top_w.T.reshape(n_pairs)
    tok_flat = jnp.tile(jnp.arange(t, dtype=jnp.int32), TOP_K)
    onehot = (e_flat[:, None] == jnp.arange(n_experts, dtype=jnp.int32)[None, :]).astype(jnp.int32)
    rank = jnp.sum((jnp.cumsum(onehot, axis=0) - onehot) * onehot, axis=1)
    counts = jnp.sum(onehot, axis=0)
    tiles_per = (counts + tm - 1) // tm
    tile_end = jnp.cumsum(tiles_per)
    row_start = (tile_end - tiles_per) * tm
    pos = (row_start[e_flat] + rank).astype(jnp.int32)
    row_token = jnp.zeros((n_rows,), jnp.int32).at[pos].set(tok_flat)
    row_gate = jnp.zeros((n_rows,), F32).at[pos].set(w_flat)
    tile_ids = jnp.arange(n_tiles, dtype=jnp.int32)
    total = tile_end[-1]
    tile_valid = (tile_ids < total).astype(jnp.int32)
    clipped = jnp.minimum(tile_ids, total - 1)
    tile_expert = jnp.minimum(
        jnp.sum((tile_end[None, :] <= clipped[:, None]).astype(jnp.int32), axis=1),
        n_experts - 1).astype(jnp.int32)
    return pos, row_token, row_gate.reshape(n_rows, 1), tile_expert, tile_valid


def kernel(x, w_in, w_out, attn_lambda_q1, attn_lambda_k1, attn_lambda_q2, attn_lambda_k2, attn_subln_w, ssm_a_re, ssm_a_im, ssm_log_dt, ssm_b_re, ssm_b_im, ssm_c_re, ssm_c_im, ssm_d, ssm_w_glu, ln1_g, ln1_b, ln2_g, ln2_b, ffn_w_gate, ffn_w_up, ffn_w_down, moe_w_router, moe_w_gate, moe_w_up, moe_w_down):
    bsz, seq, dm = x.shape
    depth = w_in.shape[0]
    d = attn_lambda_q1.shape[-1]
    ssm_width = ssm_d.shape[-1]
    attn_width = w_out.shape[1] - ssm_width
    n_heads = attn_width // (2 * d)
    n_experts = moe_w_router.shape[-1]
    t = bsz * seq
    alpha = (2.0 * depth) ** 0.25
    moe_tm = _pick(t, 512, V7X_SUBLANES * 2)

    xf = x.reshape(t, dm).astype(F32)
    xb = xf.astype(BF16)
    col_scale = jnp.concatenate([jnp.full((1, attn_width), d ** -0.5 * math.log2(math.e), F32),
                                 jnp.ones((1, 2 * attn_width), F32)], axis=1)

    for l in range(depth):
        lam_init = 0.8 - 0.6 * math.exp(-0.3 * l)
        w_in_l = w_in[l].astype(BF16)
        qkv_width = 3 * attn_width
        h = _matmul(xb, [w_in_l[:, :qkv_width]], _epi_scale_cols, BF16, tm=1024, tn=512,
                    col_extras=[col_scale], name=f"w_in_qkv_{l}")
        h = h.reshape(bsz, seq, qkv_width)
        u = _matmul(xb, [w_in_l[:, qkv_width:]], _epi_first, F32, tm=1024, tn=512,
                    name=f"w_in_u_{l}")
        lam_params = jnp.stack([attn_lambda_q1[l], attn_lambda_k1[l],
                                attn_lambda_q2[l], attn_lambda_k2[l]]).astype(F32)
        o = _diff_attention(h, lam_params, attn_subln_w[l].astype(F32), n_heads=n_heads, d=d,
                            lam_init=lam_init, tq=1024, tk=1024, tk_diag=512)
        g = _s5_gelu(u, ssm_a_re[l], ssm_a_im[l], ssm_log_dt[l], ssm_b_re[l], ssm_b_im[l],
                     ssm_c_re[l], ssm_c_im[l], ssm_d[l], bsz=bsz)
        y = _matmul(g, [ssm_w_glu[l].astype(BF16)], _epi_glu, BF16, tm=1024, tn=512,
                    row_extras=[g], name=f"ssm_glu_{l}")
        xf, xb = _matmul_ln([o.reshape(t, attn_width), y], w_out[l].astype(BF16), xf,
                            ln1_g[l].astype(F32), ln1_b[l].astype(F32), alpha=alpha,
                            tm=512, tk=512, name=f"w_out_ln_{l}")
        if l % 2 == 0:
            e = l // 2
            pad = (-ffn_w_gate.shape[-1]) % FFN_K_TILE
            w_gate = jnp.pad(ffn_w_gate[e].astype(BF16), ((0, 0), (0, pad)))
            w_up = jnp.pad(ffn_w_up[e].astype(BF16), ((0, 0), (0, pad)))
            w_down = jnp.pad(ffn_w_down[e].astype(BF16), ((0, pad), (0, 0)))
            hid = _matmul(xb, [w_gate, w_up], _epi_swiglu, BF16, tm=1024, tn=256,
                          name=f"ffn_gate_up_{l}")
            xf, xb = _matmul_ln([hid], w_down, xf, ln2_g[l].astype(F32),
                                ln2_b[l].astype(F32), alpha=alpha, tm=512, tk=FFN_K_TILE,
                                name=f"ffn_down_ln_{l}")
        else:
            e = l // 2
            top_idx, top_w = _router(xf, moe_w_router[e], tm=512)
            pos, row_token, row_gate, tile_expert, tile_valid = _routing_tables(
                top_idx, top_w, n_experts, moe_tm)
            hid = _moe_gate_up(xf, moe_w_gate[e].astype(BF16), moe_w_up[e].astype(BF16),
                               row_token, tile_expert, tile_valid, tm=moe_tm, tn=512)
            ys = _grouped_matmul(hid, [moe_w_down[e].astype(BF16)], tile_expert, tile_valid,
                                 _epi_row_scale, F32, tm=moe_tm, tn=1024,
                                 row_extras=[row_gate], name=f"moe_down_{l}")
            xf, xb = _combine_ln(ys, pos, xf, ln2_g[l].astype(F32), ln2_b[l].astype(F32),
                                 alpha=alpha, tc=256)
    return xf.reshape(bsz, seq, dm).astype(x.dtype)
```

```python
import functools
import math

import jax
import jax.numpy as jnp
from jax import lax
from jax.experimental import pallas as pl
from jax.experimental.pallas import tpu as pltpu

F32 = jnp.float32
BF16 = jnp.bfloat16

V7X_LANES = 128
V7X_SUBLANES = 8
V7X_VMEM_LIMIT_BYTES = 56 * 1024 * 1024

ATTN_CHUNK = 64
TOP_K = 2
LN_EPS = 1e-5
RMS_EPS = 1e-5
MASK_VALUE = -1e30
S5_CHUNK = 64
FFN_K_TILE = 512


def _pick(dim, target, align):
    if dim <= target:
        return dim
    t = (target // align) * align
    while t >= align:
        if dim % t == 0:
            return t
        t -= align
    raise ValueError(f"no tile for dim={dim} target={target} align={align}")


def _params(semantics):
    return pltpu.CompilerParams(dimension_semantics=semantics,
                                vmem_limit_bytes=V7X_VMEM_LIMIT_BYTES)


def _mm_kernel(*refs, n_rhs, n_row, n_col, epilogue, zero_from):
    lhs_ref = refs[0]
    rhs_refs = refs[1:1 + n_rhs]
    row_refs = refs[1 + n_rhs:1 + n_rhs + n_row]
    col_refs = refs[1 + n_rhs + n_row:1 + n_rhs + n_row + n_col]
    o_ref = refs[-1]

    def compute():
        a = lhs_ref[...].astype(BF16)
        accs = [jnp.dot(a, r[...], preferred_element_type=F32) for r in rhs_refs]
        out = epilogue(accs, [r[...] for r in row_refs], [c[...] for c in col_refs])
        o_ref[...] = out.astype(o_ref.dtype)

    if zero_from is None:
        compute()
    else:
        pl.when(pl.program_id(1) < zero_from)(compute)

        @pl.when(pl.program_id(1) >= zero_from)
        def _():
            o_ref[...] = jnp.zeros_like(o_ref)


def _matmul(lhs, rhss, epilogue, out_dtype, *, tm, tn, row_extras=(), col_extras=(), name,
            n_out=None, rhs_first_col=0):
    m, k = lhs.shape
    rhs_cols = rhss[0].shape[1] - rhs_first_col
    n = rhs_cols if n_out is None else n_out
    tm = _pick(m, tm, V7X_SUBLANES * 2)
    tn = _pick(math.gcd(n, rhs_cols, rhs_first_col) if rhs_first_col else math.gcd(n, rhs_cols),
               tn, V7X_LANES)
    first = rhs_first_col // tn
    valid = min(rhs_cols, n) // tn
    zero_from = valid if n > rhs_cols else None
    in_specs = [pl.BlockSpec((tm, k), lambda i, j: (i, 0))]
    in_specs += [pl.BlockSpec((k, tn), lambda i, j: (0, first + jnp.minimum(j, valid - 1)))
                 for _ in rhss]
    in_specs += [pl.BlockSpec((tm, tn), lambda i, j: (i, j)) for _ in row_extras]
    in_specs += [pl.BlockSpec((1, tn), lambda i, j: (0, j)) for _ in col_extras]
    body = functools.partial(_mm_kernel, n_rhs=len(rhss), n_row=len(row_extras),
                             n_col=len(col_extras), epilogue=epilogue, zero_from=zero_from)
    return pl.pallas_call(
        body,
        grid=(m // tm, n // tn),
        in_specs=in_specs,
        out_specs=pl.BlockSpec((tm, tn), lambda i, j: (i, j)),
        out_shape=jax.ShapeDtypeStruct((m, n), out_dtype),
        compiler_params=_params(("parallel", "arbitrary")),
        name=name,
    )(lhs, *rhss, *row_extras, *col_extras)


def _epi_scale_cols(accs, rows, cols):
    return accs[0] * cols[0]


def _epi_first(accs, rows, cols):
    return accs[0]


def _epi_swiglu(accs, rows, cols):
    g = accs[0]
    return g * jax.nn.sigmoid(g) * accs[1]


def _epi_glu(accs, rows, cols):
    g = rows[0].astype(F32)
    return g * jax.nn.sigmoid(accs[0])


def _layer_norm_rows(r, g, b):
    mu = jnp.mean(r, axis=-1, keepdims=True)
    c = r - mu
    var = jnp.mean(c * c, axis=-1, keepdims=True)
    return c * lax.rsqrt(var + LN_EPS) * g + b


def _top2_gates(x, w_pad, n_experts):
    logits = jnp.dot(x, w_pad, preferred_element_type=F32, precision=lax.Precision.HIGHEST)
    col = lax.broadcasted_iota(jnp.int32, logits.shape, 1)
    big = jnp.int32(logits.shape[1])
    lg = jnp.where(col < n_experts, logits, -jnp.inf)
    m1 = jnp.max(lg, axis=-1, keepdims=True)
    i1 = jnp.min(jnp.where(lg == m1, col, big), axis=-1, keepdims=True)
    lg2 = jnp.where(col == i1, -jnp.inf, lg)
    m2 = jnp.max(lg2, axis=-1, keepdims=True)
    i2 = jnp.min(jnp.where(lg2 == m2, col, big), axis=-1, keepdims=True)
    e = jnp.exp(m2 - m1)
    w1 = 1.0 / (1.0 + e)
    w2 = e / (1.0 + e)
    idx = jnp.where(col == 0, i1, jnp.where(col == 1, i2, 0))
    gate = jnp.where(col == 0, w1, jnp.where(col == 1, w2, 0.0))
    return idx, gate


def _mm_ln_kernel(*refs, alpha, part_blocks, n_experts):
    n_parts = len(part_blocks)
    lhs_refs = refs[:n_parts]
    rhs_ref, x_ref, g_ref, b_ref = refs[n_parts:n_parts + 4]
    if n_experts:
        wr_ref, of_ref, ob_ref, idx_ref, gate_ref = refs[n_parts + 4:]
    else:
        of_ref, ob_ref = refs[n_parts + 4:]
    kk = pl.program_id(1)

    @pl.when(kk == 0)
    def _():
        of_ref[...] = jnp.dot(lhs_refs[0][...], rhs_ref[...], preferred_element_type=F32)

    first = 0
    for lhs_ref, blocks in zip(lhs_refs, part_blocks):
        @pl.when(jnp.logical_and(kk >= max(first, 1), kk < first + blocks))
        def _(lhs_ref=lhs_ref):
            of_ref[...] += jnp.dot(lhs_ref[...], rhs_ref[...], preferred_element_type=F32)
        first += blocks

    @pl.when(kk == pl.num_programs(1) - 1)
    def _():
        y = _layer_norm_rows(alpha * x_ref[...] + of_ref[...], g_ref[...], b_ref[...])
        of_ref[...] = y
        ob_ref[...] = y.astype(BF16)
        if n_experts:
            idx_ref[...], gate_ref[...] = _top2_gates(y, wr_ref[...], n_experts)


def _matmul_ln(lhs_parts, rhs, x, g, b, *, alpha, tm, tk, name, w_router=None):
    m = lhs_parts[0].shape[0]
    n = rhs.shape[1]
    tm = _pick(m, tm, V7X_SUBLANES * 2)
    tk = _pick(math.gcd(*[p.shape[1] for p in lhs_parts]), tk, V7X_LANES)
    part_blocks = tuple(p.shape[1] // tk for p in lhs_parts)

    def part_spec(first, blocks):
        return pl.BlockSpec((tm, tk),
                            lambda i, kk: (i, jnp.clip(kk - first, 0, blocks - 1)))

    def row_block(width):
        return pl.BlockSpec((tm, width), lambda i, kk: (i, 0))

    firsts = [sum(part_blocks[:p]) for p in range(len(part_blocks))]
    in_specs = [part_spec(f, nb) for f, nb in zip(firsts, part_blocks)] + [
        pl.BlockSpec((tk, n), lambda i, kk: (kk, 0)),
        pl.BlockSpec((tm, n), lambda i, kk: (i, 0), pipeline_mode=pl.Buffered(1)),
        pl.BlockSpec((1, n), lambda i, kk: (0, 0)),
        pl.BlockSpec((1, n), lambda i, kk: (0, 0)),
    ]
    operands = [*lhs_parts, rhs, x, g.reshape(1, n), b.reshape(1, n)]
    out_specs = [row_block(n), row_block(n)]
    out_shape = [jax.ShapeDtypeStruct((m, n), F32), jax.ShapeDtypeStruct((m, n), BF16)]
    n_experts = 0
    if w_router is not None:
        n_experts = w_router.shape[1]
        w_pad = jnp.zeros((n, V7X_LANES), F32).at[:, :n_experts].set(w_router.astype(F32))
        in_specs.append(pl.BlockSpec((n, V7X_LANES), lambda i, kk: (0, 0)))
        operands.append(w_pad)
        out_specs += [row_block(V7X_LANES), row_block(V7X_LANES)]
        out_shape += [jax.ShapeDtypeStruct((m, V7X_LANES), jnp.int32),
                      jax.ShapeDtypeStruct((m, V7X_LANES), F32)]
    outs = pl.pallas_call(
        functools.partial(_mm_ln_kernel, alpha=alpha, part_blocks=part_blocks,
                          n_experts=n_experts),
        grid=(m // tm, sum(part_blocks)),
        in_specs=in_specs,
        out_specs=out_specs,
        out_shape=out_shape,
        compiler_params=_params(("parallel", "arbitrary")),
        name=name,
    )(*operands)
    if w_router is None:
        return outs
    return outs[0], outs[1], outs[2][:, :TOP_K], outs[3][:, :TOP_K]


def _attn_kernel(lam_ref, w_ref, q_ref, k_ref, v_ref, o_ref, m_sc, l_sc, acc_sc, *,
                 tq, tk, tk_diag, d, lam_init):
    qi = pl.program_id(2)
    hw = 2 * d
    lanes = V7X_LANES
    m_sc[...] = jnp.full(m_sc.shape, MASK_VALUE, F32)
    l_sc[...] = jnp.zeros_like(l_sc)
    acc_sc[...] = jnp.zeros_like(acc_sc)

    def step(start, width, diag_block):
        r0 = 0 if diag_block is None else diag_block * width
        nr = tq - r0
        start = pl.multiple_of(start, width)
        kb = k_ref[pl.ds(start, width), :]
        vb = v_ref[pl.ds(start, width), :]
        if diag_block is not None:
            row = lax.broadcasted_iota(jnp.int32, (nr, width), 0) // ATTN_CHUNK
            col = lax.broadcasted_iota(jnp.int32, (nr, width), 1) // ATTN_CHUNK
            allowed = col <= row
        scores = [lax.dot_general(q_ref[r0:tq, mp * d:(mp + 1) * d], kb[:, mp * d:(mp + 1) * d],
                                  (((1,), (1,)), ((), ())), preferred_element_type=F32)
                  for mp in range(2)]
        for mp in range(2):
            s = scores[mp]
            if diag_block is not None:
                s = jnp.where(allowed, s, MASK_VALUE)
            m_prev = m_sc[mp, r0:tq]
            m_next = jnp.maximum(m_prev, jnp.max(s, axis=-1, keepdims=True))
            alpha = jnp.exp2(m_prev - m_next)
            ps = [jnp.exp2(s[:, c * lanes:(c + 1) * lanes] - m_next) for c in range(width // lanes)]
            psum = ps[0]
            for pc in ps[1:]:
                psum = psum + pc
            l_sc[mp, r0:tq] = alpha * l_sc[mp, r0:tq] + psum
            p = jnp.concatenate(ps, axis=1).astype(BF16)
            alpha_w = jnp.concatenate([alpha] * (hw // lanes), axis=1)
            acc_sc[mp, r0:tq] = alpha_w * acc_sc[mp, r0:tq] + jnp.dot(
                p, vb, preferred_element_type=F32)
            m_sc[mp, r0:tq] = m_next

    def body(j, carry):
        step(j * tk, tk, None)
        return carry

    lax.fori_loop(0, qi * (tq // tk), body, 0)
    for c in range(tq // tk_diag):
        step(qi * tq + c * tk_diag, tk_diag, c)

    lam_p = lam_ref[...]
    lam = (jnp.exp(jnp.sum(lam_p[0:1] * lam_p[1:2], axis=-1, keepdims=True))
           - jnp.exp(jnp.sum(lam_p[2:3] * lam_p[3:4], axis=-1, keepdims=True)) + lam_init)
    l0 = jnp.sum(l_sc[0], axis=-1, keepdims=True)
    l1 = jnp.sum(l_sc[1], axis=-1, keepdims=True)
    o = acc_sc[0] / l0 - lam * (acc_sc[1] / l1)
    o = o * lax.rsqrt(jnp.mean(o * o, axis=-1, keepdims=True) + RMS_EPS)
    o_ref[...] = (o * w_ref[...] * (1.0 - lam_init)).astype(o_ref.dtype)


def _diff_attention(h, lam_params, subln_w, *, n_heads, d, lam_init, tq, tk, tk_diag):
    bsz, seq, _ = h.shape
    hw = 2 * d
    tq = _pick(seq, tq, V7X_LANES)
    tk = _pick(tq, tk, V7X_LANES)
    tk_diag = _pick(tq, tk_diag, V7X_LANES)
    return pl.pallas_call(
        functools.partial(_attn_kernel, tq=tq, tk=tk, tk_diag=tk_diag, d=d, lam_init=lam_init),
        grid=(bsz, n_heads, seq // tq),
        in_specs=[
            pl.BlockSpec((4, d), lambda b, hd, i: (0, 0)),
            pl.BlockSpec((1, hw), lambda b, hd, i: (0, 0)),
            pl.BlockSpec((None, tq, hw), lambda b, hd, i: (b, i, hd)),
            pl.BlockSpec((None, seq, hw), lambda b, hd, i: (b, 0, n_heads + hd)),
            pl.BlockSpec((None, seq, hw), lambda b, hd, i: (b, 0, 2 * n_heads + hd)),
        ],
        out_specs=pl.BlockSpec((None, tq, hw), lambda b, hd, i: (b, i, hd)),
        out_shape=jax.ShapeDtypeStruct((bsz, seq, n_heads * hw), BF16),
        scratch_shapes=[pltpu.VMEM((2, tq, V7X_LANES), F32), pltpu.VMEM((2, tq, V7X_LANES), F32),
                        pltpu.VMEM((2, tq, hw), F32)],
        compiler_params=_params(("parallel", "parallel", "arbitrary")),
        name="diff_attention",
    )(lam_params, subln_w.reshape(1, hw), h, h, h)


def _s5_kernel(u_ref, acol_ref, arow_ref, bt_ref, bx_ref, ct_ref, d_ref, perm_ref, permt_ref,
               o_ref, m_ref, u_all, y_all, *, tc, p, n, n_chunks, rows):
    tcp = tc * p
    lanes = V7X_LANES
    spb = lanes // p
    gpb = lanes // p
    n_blk = tcp // lanes
    gi = pl.program_id(1)

    @pl.when(gi == 0)
    def _():
        for k in range(n_blk):
            zk = jnp.concatenate(
                [u_ref[pl.ds(k * spb + s, rows, stride=tc), :] for s in range(spb)], axis=1)
            ok = jnp.dot(zk.astype(BF16), perm_ref[...], preferred_element_type=F32)
            for g2 in range(gpb):
                u_all[g2, :, k * lanes:(k + 1) * lanes] = (
                    ok[:, g2 * lanes:(g2 + 1) * lanes].astype(BF16))

    u = u_all[gi]

    ar = acol_ref[:, 0:1]
    ai = acol_ref[:, 1:2]
    dt = acol_ref[:, 2:3]
    lane_id = lax.broadcasted_iota(jnp.int32, (1, lanes), 1)
    tau0 = (lane_id // p).astype(F32)

    def powers(t):
        mag = jnp.exp((ar * dt) * t)
        ang = (ai * dt) * t
        return mag * jnp.cos(ang), mag * jnp.sin(ang)

    def cmul2(a_re, a_im, b_re, b_im):
        return a_re * b_re - a_im * b_im, a_re * b_im + a_im * b_re

    mag1 = jnp.exp(ar * dt)
    lr = mag1 * jnp.cos(ai * dt)
    li = mag1 * jnp.sin(ai * dt)
    den = ar * ar + ai * ai
    f_re = ((lr - 1.0) * ar + li * ai) / den
    f_im = (li * ar - (lr - 1.0) * ai) / den
    bb_re, bb_im = cmul2(f_re, f_im, bt_ref[0], bt_ref[1])

    blk_re, blk_im = powers((lane_id * spb).astype(F32))
    zb_re, zb_im = cmul2(*powers((spb - 1.0) - tau0), bb_re, bb_im)
    vb_re, vb_im = cmul2(*powers(tau0), ct_ref[0], ct_ref[1])
    z_blocks, v_blocks = [], []
    for k in range(n_blk):
        kr = n_blk - 1 - k
        z_blocks.append(cmul2(zb_re, zb_im, blk_re[:, kr:kr + 1], blk_im[:, kr:kr + 1]))
        v_blocks.append(cmul2(vb_re, vb_im, blk_re[:, k:k + 1], blk_im[:, k:k + 1]))
    z_re = jnp.concatenate([z[0] for z in z_blocks], axis=1)
    z_im = jnp.concatenate([z[1] for z in z_blocks], axis=1)
    wt = jnp.concatenate([z_re, z_im], axis=0).astype(BF16)
    v_re = jnp.concatenate([v[0] for v in v_blocks], axis=1)
    v_im = jnp.concatenate([v[1] for v in v_blocks], axis=1)
    vmat = jnp.concatenate([v_re, -v_im], axis=0)

    ar2 = arow_ref[0:1, :]
    ai2 = arow_ref[1:2, :]
    dt2 = arow_ref[2:3, :]
    half = lax.broadcasted_iota(jnp.int32, (1, 2 * n), 1) < n
    sgn = jnp.where(half, -1.0, 1.0).astype(F32)
    mag1r = jnp.exp(ar2 * dt2)
    lr2 = mag1r * jnp.cos(ai2 * dt2)
    li2 = mag1r * jnp.sin(ai2 * dt2)
    den2 = ar2 * ar2 + ai2 * ai2
    fr2 = ((lr2 - 1.0) * ar2 + li2 * ai2) / den2
    fi2 = (li2 * ar2 - (lr2 - 1.0) * ai2) / den2
    bbt = fr2 * bx_ref[0] + (sgn * fi2) * bx_ref[1]
    kq = jnp.dot(bbt, vmat, preferred_element_type=F32, precision=lax.Precision.HIGHEST)

    lane = lax.broadcasted_iota(jnp.int32, (p, tcp), 1)
    m_ref[0:p, :] = kq.astype(BF16)
    for s in range(1, tc):
        blk = jnp.where(lane >= s * p, pltpu.roll(kq, s * p, axis=1), 0.0)
        m_ref[s * p:(s + 1) * p, :] = blk.astype(BF16)

    def cmul(x, r2, i2s):
        return x * r2 + pltpu.roll(x, n, axis=1) * i2s

    st = lax.dot_general(u, wt, (((1,), (1,)), ((), ())), preferred_element_type=F32)
    cidx = lax.broadcasted_iota(jnp.int32, (rows, 2 * n), 0) % n_chunks
    mag_c = jnp.exp(ar2 * dt2 * tc)
    ang_c = ai2 * dt2 * tc
    r2 = mag_c * jnp.cos(ang_c)
    i2 = sgn * mag_c * jnp.sin(ang_c)
    k = 1
    while k < n_chunks:
        sh = jnp.where(cidx >= k, pltpu.roll(st, k, axis=0), 0.0)
        st = st + cmul(sh, r2, i2)
        r2, i2 = r2 * r2 - i2 * i2, 2.0 * r2 * i2
        k *= 2
    hin = jnp.where(cidx >= 1, pltpu.roll(st, 1, axis=0), 0.0)
    hp = cmul(hin, lr2, sgn * li2)

    y = jnp.dot(u, m_ref[...], preferred_element_type=F32)
    y = y + jnp.dot(hp.astype(BF16), vmat.astype(BF16), preferred_element_type=F32)
    y = y + u.astype(F32) * d_ref[...]
    g = 0.5 * y * (1.0 + lax.erf(y * (2.0 ** -0.5)))
    y_all[gi] = g.astype(BF16)

    @pl.when(gi == gpb - 1)
    def _():
        for k in range(n_blk):
            wk = jnp.concatenate(
                [y_all[g2, :, k * lanes:(k + 1) * lanes] for g2 in range(gpb)], axis=1)
            ok = jnp.dot(wk, permt_ref[...], preferred_element_type=F32)
            for s in range(spb):
                o_ref[pl.ds(k * spb + s, rows, stride=tc), :] = ok[:, s * lanes:(s + 1) * lanes]


def _s5_gelu(u, a_re, a_im, log_dt, b_re, b_im, c_re, c_im, d_skip, *, bsz):
    t, width = u.shape
    seq = t // bsz
    n_groups, n = a_re.shape
    p = b_re.shape[-1]
    tc = S5_CHUNK
    n_chunks = seq // tc
    rows = bsz * n_chunks
    tcp = tc * p
    lanes = V7X_LANES
    gpb = lanes // p
    dt = jnp.exp(log_dt.astype(F32))
    dtb = jnp.broadcast_to(dt[:, None], (n_groups, n))
    acol = jnp.stack([a_re.astype(F32), a_im.astype(F32), dtb], axis=-1)
    arow = jnp.stack([jnp.tile(a_re.astype(F32), (1, 2)), jnp.tile(a_im.astype(F32), (1, 2)),
                      jnp.tile(dtb, (1, 2))], axis=1)
    spb = lanes // p
    bt = jnp.stack([jnp.tile(b_re.astype(F32), (1, 1, spb)),
                    jnp.tile(b_im.astype(F32), (1, 1, spb))], axis=1)
    btr = jnp.swapaxes(b_re.astype(F32), 1, 2)
    bti = jnp.swapaxes(b_im.astype(F32), 1, 2)
    bx = jnp.stack([jnp.concatenate([btr, bti], axis=-1),
                    jnp.concatenate([bti, btr], axis=-1)], axis=1)
    ctr = jnp.swapaxes(c_re.astype(F32), 1, 2)
    cti = jnp.swapaxes(c_im.astype(F32), 1, 2)
    ct = jnp.stack([jnp.tile(ctr, (1, 1, spb)), jnp.tile(cti, (1, 1, spb))], axis=1)
    dtile = jnp.tile(d_skip.astype(F32).reshape(n_groups, 1, p), (1, 1, tc))

    src = jnp.arange(spb * lanes, dtype=jnp.int32)
    dst = ((src % lanes) // p) * (spb * p) + (src // lanes) * p + src % p
    perm = (dst[:, None] == src[None, :]).astype(BF16)
    permt = perm.T

    def grp(*tail):
        return lambda cb, gi: (cb * gpb + gi,) + tail

    return pl.pallas_call(
        functools.partial(_s5_kernel, tc=tc, p=p, n=n, n_chunks=n_chunks, rows=rows),
        grid=(n_groups // gpb, gpb),
        in_specs=[
            pl.BlockSpec((t, lanes), lambda cb, gi: (0, cb)),
            pl.BlockSpec((None, n, 3), grp(0, 0)),
            pl.BlockSpec((None, 3, 2 * n), grp(0, 0)),
            pl.BlockSpec((None, 2, n, lanes), grp(0, 0, 0)),
            pl.BlockSpec((None, 2, p, 2 * n), grp(0, 0, 0)),
            pl.BlockSpec((None, 2, n, lanes), grp(0, 0, 0)),
            pl.BlockSpec((None, 1, tcp), grp(0, 0)),
            pl.BlockSpec((spb * lanes, spb * lanes), lambda cb, gi: (0, 0)),
            pl.BlockSpec((spb * lanes, spb * lanes), lambda cb, gi: (0, 0)),
        ],
        out_specs=pl.BlockSpec((t, lanes), lambda cb, gi: (0, cb)),
        out_shape=jax.ShapeDtypeStruct((t, width), F32),
        scratch_shapes=[pltpu.VMEM((tcp, tcp), BF16), pltpu.VMEM((gpb, rows, tcp), BF16),
                        pltpu.VMEM((gpb, rows, tcp), BF16)],
        compiler_params=_params(("parallel", "arbitrary")),
        name="s5_gelu",
    )(u, acol, arow, bt, bx, ct, dtile, perm, permt)


GATHER_UNROLL = 8


def _moe_gate_up_kernel(te_ref, tv_ref, tok_ref, x_hbm, wg_ref, wu_ref, o_ref, xbuf, xb, sem, *,
                        tm):
    i = pl.program_id(0)
    j = pl.program_id(1)
    n_tiles = pl.num_programs(0)
    slot = i % 2
    valid = tv_ref[i] != 0

    def issue_tile(tile, dst_slot):
        def issue(r, carry):
            pltpu.make_async_copy(x_hbm.at[pl.ds(tok_ref[tile * tm + r], 1), :],
                                  xbuf.at[dst_slot, pl.ds(r, 1), :], sem.at[dst_slot]).start()
            return carry
        lax.fori_loop(0, tm, issue, 0, unroll=GATHER_UNROLL)

    def wait_tile(dst_slot):
        pltpu.make_async_copy(x_hbm.at[pl.ds(0, tm), :], xbuf.at[dst_slot],
                              sem.at[dst_slot]).wait()

    @pl.when(jnp.logical_and(j == 0, i == 0))
    def _():
        issue_tile(0, 0)

    @pl.when(jnp.logical_and(j == 0, valid))
    def _():
        wait_tile(slot)

    nxt = jnp.minimum(i + 1, n_tiles - 1)

    @pl.when(jnp.logical_and(j == 0, jnp.logical_and(i + 1 < n_tiles, tv_ref[nxt] != 0)))
    def _():
        issue_tile(i + 1, 1 - slot)

    @pl.when(jnp.logical_and(j == 0, valid))
    def _():
        xb[...] = xbuf[slot].astype(BF16)

    @pl.when(valid)
    def _():
        a = xb[...]
        gate = jnp.dot(a, wg_ref[...], preferred_element_type=F32)
        up = jnp.dot(a, wu_ref[...], preferred_element_type=F32)
        o_ref[...] = _epi_swiglu([gate, up], [], []).astype(o_ref.dtype)

    @pl.when(jnp.logical_not(valid))
    def _():
        o_ref[...] = jnp.zeros_like(o_ref)


def _moe_gate_up(x, w_gate, w_up, row_token, tile_expert, tile_valid, *, tm, tn):
    k = x.shape[1]
    n_rows = row_token.shape[0]
    n = w_gate.shape[2]
    tn = _pick(n, tn, V7X_LANES)
    return pl.pallas_call(
        functools.partial(_moe_gate_up_kernel, tm=tm),
        grid_spec=pltpu.PrefetchScalarGridSpec(
            num_scalar_prefetch=3,
            grid=(n_rows // tm, n // tn),
            in_specs=[pl.BlockSpec(memory_space=pl.ANY),
                      pl.BlockSpec((None, k, tn), lambda i, j, te, tv, tok: (te[i], 0, j)),
                      pl.BlockSpec((None, k, tn), lambda i, j, te, tv, tok: (te[i], 0, j))],
            out_specs=pl.BlockSpec((tm, tn), lambda i, j, te, tv, tok: (i, j)),
            scratch_shapes=[pltpu.VMEM((2, tm, k), x.dtype), pltpu.VMEM((tm, k), BF16),
                            pltpu.SemaphoreType.DMA((2,))],
        ),
        out_shape=jax.ShapeDtypeStruct((n_rows, n), BF16),
        compiler_params=_params(("arbitrary", "arbitrary")),
        name="moe_gate_up",
    )(tile_expert, tile_valid, row_token, x, w_gate, w_up)


def _grouped_kernel(te_ref, tv_ref, *refs, n_rhs, n_row, epilogue):
    lhs_ref = refs[0]
    rhs_refs = refs[1:1 + n_rhs]
    row_refs = refs[1 + n_rhs:1 + n_rhs + n_row]
    o_ref = refs[-1]

    valid = tv_ref[pl.program_id(0)] != 0

    @pl.when(valid)
    def _():
        a = lhs_ref[...]
        accs = [jnp.dot(a, r[...], preferred_element_type=F32) for r in rhs_refs]
        o_ref[...] = epilogue(accs, [r[...] for r in row_refs], []).astype(o_ref.dtype)

    @pl.when(jnp.logical_not(valid))
    def _():
        o_ref[...] = jnp.zeros_like(o_ref)


def _grouped_matmul(lhs, rhss, tile_expert, tile_valid, epilogue, out_dtype, *, tm, tn,
                    row_extras=(), name):
    r, k = lhs.shape
    n = rhss[0].shape[2]
    tn = _pick(n, tn, V7X_LANES)
    in_specs = [pl.BlockSpec((tm, k), lambda i, j, te, tv: (i, 0))]
    in_specs += [pl.BlockSpec((None, k, tn), lambda i, j, te, tv: (te[i], 0, j)) for _ in rhss]
    in_specs += [pl.BlockSpec((tm, 1), lambda i, j, te, tv: (i, 0)) for _ in row_extras]
    return pl.pallas_call(
        functools.partial(_grouped_kernel, n_rhs=len(rhss), n_row=len(row_extras),
                          epilogue=epilogue),
        grid_spec=pltpu.PrefetchScalarGridSpec(
            num_scalar_prefetch=2,
            grid=(r // tm, n // tn),
            in_specs=in_specs,
            out_specs=pl.BlockSpec((tm, tn), lambda i, j, te, tv: (i, j)),
        ),
        out_shape=jax.ShapeDtypeStruct((r, n), out_dtype),
        compiler_params=_params(("parallel", "arbitrary")),
        name=name,
    )(tile_expert, tile_valid, lhs, *rhss, *row_extras)


def _epi_row_scale(accs, rows, cols):
    return accs[0] * rows[0]


def _combine_ln_kernel(pos_ref, y_hbm, x_ref, g_ref, b_ref, of_ref, ob_ref, buf, sem, *,
                       tc, n_tokens, alpha):
    i = pl.program_id(0)
    slot = i % 2

    def issue_tile(tile, dst_slot):
        base = tile * tc

        def issue(r, carry):
            for kk in range(TOP_K):
                pltpu.make_async_copy(y_hbm.at[pl.ds(pos_ref[kk * n_tokens + base + r], 1), :],
                                      buf.at[dst_slot, kk, pl.ds(r, 1), :],
                                      sem.at[dst_slot]).start()
            return carry
        lax.fori_loop(0, tc, issue, 0, unroll=GATHER_UNROLL)

    @pl.when(i == 0)
    def _():
        issue_tile(0, 0)

    for kk in range(TOP_K):
        pltpu.make_async_copy(y_hbm.at[pl.ds(0, tc), :], buf.at[slot, kk], sem.at[slot]).wait()

    @pl.when(i + 1 < pl.num_programs(0))
    def _():
        issue_tile(i + 1, 1 - slot)

    f = buf[slot, 0]
    for kk in range(1, TOP_K):
        f = f + buf[slot, kk]
    y = _layer_norm_rows(alpha * x_ref[...] + f, g_ref[...], b_ref[...])
    of_ref[...] = y
    ob_ref[...] = y.astype(BF16)


def _combine_ln(y_sorted, pos, x, g, b, *, alpha, tc):
    t, dm = x.shape
    tc = _pick(t, tc, V7X_SUBLANES * 2)
    return pl.pallas_call(
        functools.partial(_combine_ln_kernel, tc=tc, n_tokens=t, alpha=alpha),
        grid_spec=pltpu.PrefetchScalarGridSpec(
            num_scalar_prefetch=1,
            grid=(t // tc,),
            in_specs=[pl.BlockSpec(memory_space=pl.ANY),
                      pl.BlockSpec((tc, dm), lambda i, pos: (i, 0)),
                      pl.BlockSpec((1, dm), lambda i, pos: (0, 0)),
                      pl.BlockSpec((1, dm), lambda i, pos: (0, 0))],
            out_specs=[pl.BlockSpec((tc, dm), lambda i, pos: (i, 0)),
                       pl.BlockSpec((tc, dm), lambda i, pos: (i, 0))],
            scratch_shapes=[pltpu.VMEM((2, TOP_K, tc, dm), y_sorted.dtype),
                            pltpu.SemaphoreType.DMA((2,))],
        ),
        out_shape=[jax.ShapeDtypeStruct((t, dm), F32), jax.ShapeDtypeStruct((t, dm), BF16)],
        compiler_params=_params(("arbitrary",)),
        name="moe_combine_ln",
    )(pos, y_sorted, x, g.reshape(1, dm), b.reshape(1, dm))


def _routing_tables(top_idx, top_w, n_experts, tm):
    t = top_idx.shape[0]
    n_pairs = TOP_K * t
    n_rows = n_pairs + n_experts * tm
    n_tiles = n_rows // tm
    e_flat = top_idx.T.reshape(n_pairs)
    w_flat = top_w.T.reshape(n_pairs)
    onehot = (e_flat[:, None] == jnp.arange(n_experts, dtype=jnp.int32)[None, :]).astype(jnp.int32)
    rank = jnp.sum((jnp.cumsum(onehot, axis=0) - onehot) * onehot, axis=1)
    counts = jnp.sum(onehot, axis=0)
    tiles_per = (counts + tm - 1) // tm
    tile_end = jnp.cumsum(tiles_per)
    row_start = (tile_end - tiles_per) * tm
    pos = (row_start[e_flat] + rank).astype(jnp.int32)
    row_pair = jnp.full((n_rows,), -1, jnp.int32).at[pos].set(
        jnp.arange(n_pairs, dtype=jnp.int32))
    has_pair = row_pair >= 0
    row_token = jnp.where(has_pair, row_pair % t, 0)
    row_gate = jnp.where(has_pair, w_flat[jnp.maximum(row_pair, 0)], 0.0)
    tile_ids = jnp.arange(n_tiles, dtype=jnp.int32)
    total = tile_end[-1]
    tile_valid = (tile_ids < total).astype(jnp.int32)
    clipped = jnp.minimum(tile_ids, total - 1)
    tile_expert = jnp.minimum(
        jnp.sum((tile_end[None, :] <= clipped[:, None]).astype(jnp.int32), axis=1),
        n_experts - 1).astype(jnp.int32)
    return pos, row_token, row_gate.reshape(n_rows, 1), tile_expert, tile_valid


def kernel(x, w_in, w_out, attn_lambda_q1, attn_lambda_k1, attn_lambda_q2, attn_lambda_k2, attn_subln_w, ssm_a_re, ssm_a_im, ssm_log_dt, ssm_b_re, ssm_b_im, ssm_c_re, ssm_c_im, ssm_d, ssm_w_glu, ln1_g, ln1_b, ln2_g, ln2_b, ffn_w_gate, ffn_w_up, ffn_w_down, moe_w_router, moe_w_gate, moe_w_up, moe_w_down):
    bsz, seq, dm = x.shape
    depth = w_in.shape[0]
    d = attn_lambda_q1.shape[-1]
    ssm_width = ssm_d.shape[-1]
    attn_width = w_out.shape[1] - ssm_width
    n_heads = attn_width // (2 * d)
    n_experts = moe_w_router.shape[-1]
    t = bsz * seq
    alpha = (2.0 * depth) ** 0.25
    moe_tm = _pick(t, 512, V7X_SUBLANES * 2)

    xf = x.reshape(t, dm).astype(F32)
    xb = xf.astype(BF16)
    col_scale = jnp.concatenate([jnp.full((1, attn_width), d ** -0.5 * math.log2(math.e), F32),
                                 jnp.ones((1, 2 * attn_width), F32)], axis=1)

    for l in range(depth):
        lam_init = 0.8 - 0.6 * math.exp(-0.3 * l)
        w_in_l = w_in[l].astype(BF16)
        qkv_width = 3 * attn_width
        h = _matmul(xb, [w_in_l], _epi_scale_cols, BF16, tm=1024, tn=512,
                    col_extras=[col_scale], n_out=qkv_width, name=f"w_in_qkv_{l}")
        h = h.reshape(bsz, seq, qkv_width)
        u = _matmul(xb, [w_in_l], _epi_first, F32, tm=1024, tn=512, rhs_first_col=qkv_width,
                    name=f"w_in_u_{l}")
        lam_params = jnp.stack([attn_lambda_q1[l], attn_lambda_k1[l],
                                attn_lambda_q2[l], attn_lambda_k2[l]]).astype(F32)
        o = _diff_attention(h, lam_params, attn_subln_w[l].astype(F32), n_heads=n_heads, d=d,
                            lam_init=lam_init, tq=1024, tk=1024, tk_diag=512)
        g = _s5_gelu(u, ssm_a_re[l], ssm_a_im[l], ssm_log_dt[l], ssm_b_re[l], ssm_b_im[l],
                     ssm_c_re[l], ssm_c_im[l], ssm_d[l], bsz=bsz)
        y = _matmul(g, [ssm_w_glu[l].astype(BF16)], _epi_glu, BF16, tm=1024, tn=512,
                    row_extras=[g], name=f"ssm_glu_{l}")
        is_moe = l % 2 == 1
        mixed = _matmul_ln([o.reshape(t, attn_width), y], w_out[l].astype(BF16), xf,
                           ln1_g[l].astype(F32), ln1_b[l].astype(F32), alpha=alpha,
                           tm=512, tk=512, name=f"w_out_ln_{l}",
                           w_router=moe_w_router[l // 2] if is_moe else None)
        xf, xb = mixed[0], mixed[1]
        if not is_moe:
            e = l // 2
            d_ff = ffn_w_gate.shape[-1]
            pad = (-d_ff) % FFN_K_TILE
            w_down = jnp.pad(ffn_w_down[e].astype(BF16), ((0, pad), (0, 0)))
            hid = _matmul(xb, [ffn_w_gate[e].astype(BF16), ffn_w_up[e].astype(BF16)],
                          _epi_swiglu, BF16, tm=1024, tn=256, n_out=d_ff + pad,
                          name=f"ffn_gate_up_{l}")
            xf, xb = _matmul_ln([hid], w_down, xf, ln2_g[l].astype(F32),
                                ln2_b[l].astype(F32), alpha=alpha, tm=512, tk=FFN_K_TILE,
                                name=f"ffn_down_ln_{l}")
        else:
            e = l // 2
            top_idx, top_w = mixed[2], mixed[3]
            pos, row_token, row_gate, tile_expert, tile_valid = _routing_tables(
                top_idx, top_w, n_experts, moe_tm)
            hid = _moe_gate_up(xf, moe_w_gate[e].astype(BF16), moe_w_up[e].astype(BF16),
                               row_token, tile_expert, tile_valid, tm=moe_tm, tn=512)
            ys = _grouped_matmul(hid, [moe_w_down[e].astype(BF16)], tile_expert, tile_valid,
                                 _epi_row_scale, F32, tm=moe_tm, tn=1024,
                                 row_extras=[row_gate], name=f"moe_down_{l}")
            xf, xb = _combine_ln(ys, pos, xf, ln2_g[l].astype(F32), ln2_b[l].astype(F32),
                                 alpha=alpha, tc=256)
    return xf.reshape(bsz, seq, dm).astype(x.dtype)
```

```python
import functools
import math

import jax
import jax.numpy as jnp
from jax import lax
from jax.experimental import pallas as pl
from jax.experimental.pallas import tpu as pltpu

F32 = jnp.float32
BF16 = jnp.bfloat16

V7X_LANES = 128
V7X_SUBLANES = 8
V7X_VMEM_LIMIT_BYTES = 56 * 1024 * 1024

ATTN_CHUNK = 64
TOP_K = 2
LN_EPS = 1e-5
RMS_EPS = 1e-5
MASK_VALUE = -1e30
S5_CHUNK = 64
FFN_K_TILE = 512


def _pick(dim, target, align):
    if dim <= target:
        return dim
    t = (target // align) * align
    while t >= align:
        if dim % t == 0:
            return t
        t -= align
    raise ValueError(f"no tile for dim={dim} target={target} align={align}")


def _params(semantics):
    return pltpu.CompilerParams(dimension_semantics=semantics,
                                vmem_limit_bytes=V7X_VMEM_LIMIT_BYTES)


def _mm_kernel(*refs, n_rhs, n_row, n_col, epilogue, zero_from):
    lhs_ref = refs[0]
    rhs_refs = refs[1:1 + n_rhs]
    row_refs = refs[1 + n_rhs:1 + n_rhs + n_row]
    col_refs = refs[1 + n_rhs + n_row:1 + n_rhs + n_row + n_col]
    o_ref = refs[-1]

    def compute():
        a = lhs_ref[...].astype(BF16)
        accs = [jnp.dot(a, r[...], preferred_element_type=F32) for r in rhs_refs]
        out = epilogue(accs, [r[...] for r in row_refs], [c[...] for c in col_refs])
        o_ref[...] = out.astype(o_ref.dtype)

    if zero_from is None:
        compute()
    else:
        pl.when(pl.program_id(1) < zero_from)(compute)

        @pl.when(pl.program_id(1) >= zero_from)
        def _():
            o_ref[...] = jnp.zeros_like(o_ref)


def _matmul(lhs, rhss, epilogue, out_dtype, *, tm, tn, row_extras=(), col_extras=(), name,
            n_out=None, rhs_first_col=0):
    m, k = lhs.shape
    rhs_cols = rhss[0].shape[1] - rhs_first_col
    n = rhs_cols if n_out is None else n_out
    tm = _pick(m, tm, V7X_SUBLANES * 2)
    tn = _pick(math.gcd(n, rhs_cols, rhs_first_col) if rhs_first_col else math.gcd(n, rhs_cols),
               tn, V7X_LANES)
    first = rhs_first_col // tn
    valid = min(rhs_cols, n) // tn
    zero_from = valid if n > rhs_cols else None
    in_specs = [pl.BlockSpec((tm, k), lambda i, j: (i, 0))]
    in_specs += [pl.BlockSpec((k, tn), lambda i, j: (0, first + jnp.minimum(j, valid - 1)))
                 for _ in rhss]
    in_specs += [pl.BlockSpec((tm, tn), lambda i, j: (i, j)) for _ in row_extras]
    in_specs += [pl.BlockSpec((1, tn), lambda i, j: (0, j)) for _ in col_extras]
    body = functools.partial(_mm_kernel, n_rhs=len(rhss), n_row=len(row_extras),
                             n_col=len(col_extras), epilogue=epilogue, zero_from=zero_from)
    return pl.pallas_call(
        body,
        grid=(m // tm, n // tn),
        in_specs=in_specs,
        out_specs=pl.BlockSpec((tm, tn), lambda i, j: (i, j)),
        out_shape=jax.ShapeDtypeStruct((m, n), out_dtype),
        compiler_params=_params(("parallel", "arbitrary")),
        name=name,
    )(lhs, *rhss, *row_extras, *col_extras)


def _epi_scale_cols(accs, rows, cols):
    return accs[0] * cols[0]


def _epi_first(accs, rows, cols):
    return accs[0]


def _epi_swiglu(accs, rows, cols):
    g = accs[0]
    return g * jax.nn.sigmoid(g) * accs[1]


def _epi_glu(accs, rows, cols):
    g = rows[0].astype(F32)
    return g * jax.nn.sigmoid(accs[0])


def _layer_norm_rows(r, g, b):
    mu = jnp.mean(r, axis=-1, keepdims=True)
    c = r - mu
    var = jnp.mean(c * c, axis=-1, keepdims=True)
    return c * lax.rsqrt(var + LN_EPS) * g + b


def _top2_gates(x, w_pad, n_experts):
    logits = jnp.dot(x, w_pad, preferred_element_type=F32, precision=lax.Precision.HIGHEST)
    col = lax.broadcasted_iota(jnp.int32, logits.shape, 1)
    big = jnp.int32(logits.shape[1])
    lg = jnp.where(col < n_experts, logits, -jnp.inf)
    m1 = jnp.max(lg, axis=-1, keepdims=True)
    i1 = jnp.min(jnp.where(lg == m1, col, big), axis=-1, keepdims=True)
    lg2 = jnp.where(col == i1, -jnp.inf, lg)
    m2 = jnp.max(lg2, axis=-1, keepdims=True)
    i2 = jnp.min(jnp.where(lg2 == m2, col, big), axis=-1, keepdims=True)
    e = jnp.exp(m2 - m1)
    w1 = 1.0 / (1.0 + e)
    w2 = e / (1.0 + e)
    idx = jnp.where(col == 0, i1, jnp.where(col == 1, i2, 0))
    gate = jnp.where(col == 0, w1, jnp.where(col == 1, w2, 0.0))
    return idx, gate


def _mm_ln_kernel(*refs, alpha, part_blocks, n_experts):
    n_parts = len(part_blocks)
    lhs_refs = refs[:n_parts]
    rhs_ref, x_hbm, g_ref, b_ref = refs[n_parts:n_parts + 4]
    if n_experts:
        wr_ref, of_ref, ob_ref, idx_ref, gate_ref, xbuf, xsem = refs[n_parts + 4:]
    else:
        of_ref, ob_ref, xbuf, xsem = refs[n_parts + 4:]
    kk = pl.program_id(1)
    tm = xbuf.shape[0]
    x_copy = pltpu.make_async_copy(x_hbm.at[pl.ds(pl.program_id(0) * tm, tm), :], xbuf, xsem)

    @pl.when(kk == 0)
    def _():
        x_copy.start()

    @pl.when(kk == 0)
    def _():
        of_ref[...] = jnp.dot(lhs_refs[0][...], rhs_ref[...], preferred_element_type=F32)

    first = 0
    for lhs_ref, blocks in zip(lhs_refs, part_blocks):
        @pl.when(jnp.logical_and(kk >= max(first, 1), kk < first + blocks))
        def _(lhs_ref=lhs_ref):
            of_ref[...] += jnp.dot(lhs_ref[...], rhs_ref[...], preferred_element_type=F32)
        first += blocks

    @pl.when(kk == pl.num_programs(1) - 1)
    def _():
        x_copy.wait()
        y = _layer_norm_rows(alpha * xbuf[...] + of_ref[...], g_ref[...], b_ref[...])
        of_ref[...] = y
        ob_ref[...] = y.astype(BF16)
        if n_experts:
            idx_ref[...], gate_ref[...] = _top2_gates(y, wr_ref[...], n_experts)


def _matmul_ln(lhs_parts, rhs, x, g, b, *, alpha, tm, tk, name, w_router=None):
    m = lhs_parts[0].shape[0]
    n = rhs.shape[1]
    tm = _pick(m, tm, V7X_SUBLANES * 2)
    tk = _pick(math.gcd(*[p.shape[1] for p in lhs_parts]), tk, V7X_LANES)
    part_blocks = tuple(p.shape[1] // tk for p in lhs_parts)

    def part_spec(first, blocks):
        return pl.BlockSpec((tm, tk),
                            lambda i, kk: (i, jnp.clip(kk - first, 0, blocks - 1)))

    def row_block(width):
        return pl.BlockSpec((tm, width), lambda i, kk: (i, 0))

    firsts = [sum(part_blocks[:p]) for p in range(len(part_blocks))]
    in_specs = [part_spec(f, nb) for f, nb in zip(firsts, part_blocks)] + [
        pl.BlockSpec((tk, n), lambda i, kk: (kk, 0)),
        pl.BlockSpec(memory_space=pl.ANY),
        pl.BlockSpec((1, n), lambda i, kk: (0, 0)),
        pl.BlockSpec((1, n), lambda i, kk: (0, 0)),
    ]
    operands = [*lhs_parts, rhs, x, g.reshape(1, n), b.reshape(1, n)]
    out_specs = [row_block(n), row_block(n)]
    out_shape = [jax.ShapeDtypeStruct((m, n), F32), jax.ShapeDtypeStruct((m, n), BF16)]
    n_experts = 0
    if w_router is not None:
        n_experts = w_router.shape[1]
        w_pad = jnp.zeros((n, V7X_LANES), F32).at[:, :n_experts].set(w_router.astype(F32))
        in_specs.append(pl.BlockSpec((n, V7X_LANES), lambda i, kk: (0, 0)))
        operands.append(w_pad)
        out_specs += [row_block(V7X_LANES), row_block(V7X_LANES)]
        out_shape += [jax.ShapeDtypeStruct((m, V7X_LANES), jnp.int32),
                      jax.ShapeDtypeStruct((m, V7X_LANES), F32)]
    outs = pl.pallas_call(
        functools.partial(_mm_ln_kernel, alpha=alpha, part_blocks=part_blocks,
                          n_experts=n_experts),
        grid=(m // tm, sum(part_blocks)),
        in_specs=in_specs,
        out_specs=out_specs,
        out_shape=out_shape,
        scratch_shapes=[pltpu.VMEM((tm, n), F32), pltpu.SemaphoreType.DMA(())],
        compiler_params=_params(("parallel", "arbitrary")),
        name=name,
    )(*operands)
    if w_router is None:
        return outs
    return outs[0], outs[1], outs[2][:, :TOP_K], outs[3][:, :TOP_K]


def _attn_kernel(lam_ref, w_ref, q_ref, k_ref, v_ref, o_ref, m_sc, l_sc, acc_sc, *,
                 tq, tk, tk_diag, d, lam_init):
    qi = pl.program_id(2)
    hw = 2 * d
    lanes = V7X_LANES
    m_sc[...] = jnp.full(m_sc.shape, MASK_VALUE, F32)
    l_sc[...] = jnp.zeros_like(l_sc)
    acc_sc[...] = jnp.zeros_like(acc_sc)

    def step(start, width, diag_block):
        r0 = 0 if diag_block is None else diag_block * width
        nr = tq - r0
        start = pl.multiple_of(start, width)
        kb = k_ref[pl.ds(start, width), :]
        vb = v_ref[pl.ds(start, width), :]
        if diag_block is not None:
            row = lax.broadcasted_iota(jnp.int32, (nr, width), 0) // ATTN_CHUNK
            col = lax.broadcasted_iota(jnp.int32, (nr, width), 1) // ATTN_CHUNK
            allowed = col <= row
        scores = [lax.dot_general(q_ref[r0:tq, mp * d:(mp + 1) * d], kb[:, mp * d:(mp + 1) * d],
                                  (((1,), (1,)), ((), ())), preferred_element_type=F32)
                  for mp in range(2)]
        for mp in range(2):
            s = scores[mp]
            if diag_block is not None:
                s = jnp.where(allowed, s, MASK_VALUE)
            m_prev = m_sc[mp, r0:tq]
            m_next = jnp.maximum(m_prev, jnp.max(s, axis=-1, keepdims=True))
            alpha = jnp.exp2(m_prev - m_next)
            ps = [jnp.exp2(s[:, c * lanes:(c + 1) * lanes] - m_next) for c in range(width // lanes)]
            psum = ps[0]
            for pc in ps[1:]:
                psum = psum + pc
            l_sc[mp, r0:tq] = alpha * l_sc[mp, r0:tq] + psum
            p = jnp.concatenate(ps, axis=1).astype(BF16)
            alpha_w = jnp.concatenate([alpha] * (hw // lanes), axis=1)
            acc_sc[mp, r0:tq] = alpha_w * acc_sc[mp, r0:tq] + jnp.dot(
                p, vb, preferred_element_type=F32)
            m_sc[mp, r0:tq] = m_next

    def body(j, carry):
        step(j * tk, tk, None)
        return carry

    lax.fori_loop(0, qi * (tq // tk), body, 0)
    for c in range(tq // tk_diag):
        step(qi * tq + c * tk_diag, tk_diag, c)

    lam_p = lam_ref[...]
    lam = (jnp.exp(jnp.sum(lam_p[0:1] * lam_p[1:2], axis=-1, keepdims=True))
           - jnp.exp(jnp.sum(lam_p[2:3] * lam_p[3:4], axis=-1, keepdims=True)) + lam_init)
    l0 = jnp.sum(l_sc[0], axis=-1, keepdims=True)
    l1 = jnp.sum(l_sc[1], axis=-1, keepdims=True)
    o = acc_sc[0] / l0 - lam * (acc_sc[1] / l1)
    o = o * lax.rsqrt(jnp.mean(o * o, axis=-1, keepdims=True) + RMS_EPS)
    o_ref[...] = (o * w_ref[...] * (1.0 - lam_init)).astype(o_ref.dtype)


def _diff_attention(h, lam_params, subln_w, *, n_heads, d, lam_init, tq, tk, tk_diag):
    bsz, seq, _ = h.shape
    hw = 2 * d
    tq = _pick(seq, tq, V7X_LANES)
    tk = _pick(tq, tk, V7X_LANES)
    tk_diag = _pick(tq, tk_diag, V7X_LANES)
    return pl.pallas_call(
        functools.partial(_attn_kernel, tq=tq, tk=tk, tk_diag=tk_diag, d=d, lam_init=lam_init),
        grid=(bsz, n_heads, seq // tq),
        in_specs=[
            pl.BlockSpec((4, d), lambda b, hd, i: (0, 0)),
            pl.BlockSpec((1, hw), lambda b, hd, i: (0, 0)),
            pl.BlockSpec((None, tq, hw), lambda b, hd, i: (b, i, hd)),
            pl.BlockSpec((None, seq, hw), lambda b, hd, i: (b, 0, n_heads + hd)),
            pl.BlockSpec((None, seq, hw), lambda b, hd, i: (b, 0, 2 * n_heads + hd)),
        ],
        out_specs=pl.BlockSpec((None, tq, hw), lambda b, hd, i: (b, i, hd)),
        out_shape=jax.ShapeDtypeStruct((bsz, seq, n_heads * hw), BF16),
        scratch_shapes=[pltpu.VMEM((2, tq, V7X_LANES), F32), pltpu.VMEM((2, tq, V7X_LANES), F32),
                        pltpu.VMEM((2, tq, hw), F32)],
        compiler_params=_params(("parallel", "parallel", "arbitrary")),
        name="diff_attention",
    )(lam_params, subln_w.reshape(1, hw), h, h, h)


def _s5_kernel(u_ref, acol_ref, arow_ref, bt_ref, bx_ref, ct_ref, d_ref, perm_ref, permt_ref,
               o_ref, m_ref, u_all, y_all, *, tc, p, n, n_chunks, rows):
    tcp = tc * p
    lanes = V7X_LANES
    spb = lanes // p
    gpb = lanes // p
    n_blk = tcp // lanes
    gi = pl.program_id(1)

    @pl.when(gi == 0)
    def _():
        for k in range(n_blk):
            zk = jnp.concatenate(
                [u_ref[pl.ds(k * spb + s, rows, stride=tc), :] for s in range(spb)], axis=1)
            ok = jnp.dot(zk.astype(BF16), perm_ref[...], preferred_element_type=F32)
            for g2 in range(gpb):
                u_all[g2, :, k * lanes:(k + 1) * lanes] = (
                    ok[:, g2 * lanes:(g2 + 1) * lanes].astype(BF16))

    u = u_all[gi]

    ar = acol_ref[:, 0:1]
    ai = acol_ref[:, 1:2]
    dt = acol_ref[:, 2:3]
    lane_id = lax.broadcasted_iota(jnp.int32, (1, lanes), 1)
    tau0 = (lane_id // p).astype(F32)

    def powers(t):
        mag = jnp.exp((ar * dt) * t)
        ang = (ai * dt) * t
        return mag * jnp.cos(ang), mag * jnp.sin(ang)

    def cmul2(a_re, a_im, b_re, b_im):
        return a_re * b_re - a_im * b_im, a_re * b_im + a_im * b_re

    mag1 = jnp.exp(ar * dt)
    lr = mag1 * jnp.cos(ai * dt)
    li = mag1 * jnp.sin(ai * dt)
    den = ar * ar + ai * ai
    f_re = ((lr - 1.0) * ar + li * ai) / den
    f_im = (li * ar - (lr - 1.0) * ai) / den
    bb_re, bb_im = cmul2(f_re, f_im, bt_ref[0], bt_ref[1])

    blk_re, blk_im = powers((lane_id * spb).astype(F32))
    zb_re, zb_im = cmul2(*powers((spb - 1.0) - tau0), bb_re, bb_im)
    vb_re, vb_im = cmul2(*powers(tau0), ct_ref[0], ct_ref[1])
    z_blocks, v_blocks = [], []
    for k in range(n_blk):
        kr = n_blk - 1 - k
        z_blocks.append(cmul2(zb_re, zb_im, blk_re[:, kr:kr + 1], blk_im[:, kr:kr + 1]))
        v_blocks.append(cmul2(vb_re, vb_im, blk_re[:, k:k + 1], blk_im[:, k:k + 1]))
    z_re = jnp.concatenate([z[0] for z in z_blocks], axis=1)
    z_im = jnp.concatenate([z[1] for z in z_blocks], axis=1)
    wt = jnp.concatenate([z_re, z_im], axis=0).astype(BF16)
    v_re = jnp.concatenate([v[0] for v in v_blocks], axis=1)
    v_im = jnp.concatenate([v[1] for v in v_blocks], axis=1)
    vmat = jnp.concatenate([v_re, -v_im], axis=0)

    ar2 = arow_ref[0:1, :]
    ai2 = arow_ref[1:2, :]
    dt2 = arow_ref[2:3, :]
    half = lax.broadcasted_iota(jnp.int32, (1, 2 * n), 1) < n
    sgn = jnp.where(half, -1.0, 1.0).astype(F32)
    mag1r = jnp.exp(ar2 * dt2)
    lr2 = mag1r * jnp.cos(ai2 * dt2)
    li2 = mag1r * jnp.sin(ai2 * dt2)
    den2 = ar2 * ar2 + ai2 * ai2
    fr2 = ((lr2 - 1.0) * ar2 + li2 * ai2) / den2
    fi2 = (li2 * ar2 - (lr2 - 1.0) * ai2) / den2
    bbt = fr2 * bx_ref[0] + (sgn * fi2) * bx_ref[1]
    kq = jnp.dot(bbt, vmat, preferred_element_type=F32, precision=lax.Precision.HIGHEST)

    lane = lax.broadcasted_iota(jnp.int32, (p, tcp), 1)
    m_ref[0:p, :] = kq.astype(BF16)
    for s in range(1, tc):
        blk = jnp.where(lane >= s * p, pltpu.roll(kq, s * p, axis=1), 0.0)
        m_ref[s * p:(s + 1) * p, :] = blk.astype(BF16)

    def cmul(x, r2, i2s):
        return x * r2 + pltpu.roll(x, n, axis=1) * i2s

    st = lax.dot_general(u, wt, (((1,), (1,)), ((), ())), preferred_element_type=F32)
    cidx = lax.broadcasted_iota(jnp.int32, (rows, 2 * n), 0) % n_chunks
    mag_c = jnp.exp(ar2 * dt2 * tc)
    ang_c = ai2 * dt2 * tc
    r2 = mag_c * jnp.cos(ang_c)
    i2 = sgn * mag_c * jnp.sin(ang_c)
    k = 1
    while k < n_chunks:
        sh = jnp.where(cidx >= k, pltpu.roll(st, k, axis=0), 0.0)
        st = st + cmul(sh, r2, i2)
        r2, i2 = r2 * r2 - i2 * i2, 2.0 * r2 * i2
        k *= 2
    hin = jnp.where(cidx >= 1, pltpu.roll(st, 1, axis=0), 0.0)
    hp = cmul(hin, lr2, sgn * li2)

    y = jnp.dot(u, m_ref[...], preferred_element_type=F32)
    y = y + jnp.dot(hp.astype(BF16), vmat.astype(BF16), preferred_element_type=F32)
    y = y + u.astype(F32) * d_ref[...]
    g = 0.5 * y * (1.0 + lax.erf(y * (2.0 ** -0.5)))
    y_all[gi] = g.astype(BF16)

    @pl.when(gi == gpb - 1)
    def _():
        for k in range(n_blk):
            wk = jnp.concatenate(
                [y_all[g2, :, k * lanes:(k + 1) * lanes] for g2 in range(gpb)], axis=1)
            ok = jnp.dot(wk, permt_ref[...], preferred_element_type=F32)
            for s in range(spb):
                o_ref[pl.ds(k * spb + s, rows, stride=tc), :] = ok[:, s * lanes:(s + 1) * lanes]


def _s5_gelu(u, a_re, a_im, log_dt, b_re, b_im, c_re, c_im, d_skip, *, bsz):
    t, width = u.shape
    seq = t // bsz
    n_groups, n = a_re.shape
    p = b_re.shape[-1]
    tc = S5_CHUNK
    n_chunks = seq // tc
    rows = bsz * n_chunks
    tcp = tc * p
    lanes = V7X_LANES
    gpb = lanes // p
    dt = jnp.exp(log_dt.astype(F32))
    dtb = jnp.broadcast_to(dt[:, None], (n_groups, n))
    acol = jnp.stack([a_re.astype(F32), a_im.astype(F32), dtb], axis=-1)
    arow = jnp.stack([jnp.tile(a_re.astype(F32), (1, 2)), jnp.tile(a_im.astype(F32), (1, 2)),
                      jnp.tile(dtb, (1, 2))], axis=1)
    spb = lanes // p
    bt = jnp.stack([jnp.tile(b_re.astype(F32), (1, 1, spb)),
                    jnp.tile(b_im.astype(F32), (1, 1, spb))], axis=1)
    btr = jnp.swapaxes(b_re.astype(F32), 1, 2)
    bti = jnp.swapaxes(b_im.astype(F32), 1, 2)
    bx = jnp.stack([jnp.concatenate([btr, bti], axis=-1),
                    jnp.concatenate([bti, btr], axis=-1)], axis=1)
    ctr = jnp.swapaxes(c_re.astype(F32), 1, 2)
    cti = jnp.swapaxes(c_im.astype(F32), 1, 2)
    ct = jnp.stack([jnp.tile(ctr, (1, 1, spb)), jnp.tile(cti, (1, 1, spb))], axis=1)
    dtile = jnp.tile(d_skip.astype(F32).reshape(n_groups, 1, p), (1, 1, tc))

    src = jnp.arange(spb * lanes, dtype=jnp.int32)
    dst = ((src % lanes) // p) * (spb * p) + (src // lanes) * p + src % p
    perm = (dst[:, None] == src[None, :]).astype(BF16)
    permt = perm.T

    def grp(*tail):
        return lambda cb, gi: (cb * gpb + gi,) + tail

    return pl.pallas_call(
        functools.partial(_s5_kernel, tc=tc, p=p, n=n, n_chunks=n_chunks, rows=rows),
        grid=(n_groups // gpb, gpb),
        in_specs=[
            pl.BlockSpec((t, lanes), lambda cb, gi: (0, cb)),
            pl.BlockSpec((None, n, 3), grp(0, 0)),
            pl.BlockSpec((None, 3, 2 * n), grp(0, 0)),
            pl.BlockSpec((None, 2, n, lanes), grp(0, 0, 0)),
            pl.BlockSpec((None, 2, p, 2 * n), grp(0, 0, 0)),
            pl.BlockSpec((None, 2, n, lanes), grp(0, 0, 0)),
            pl.BlockSpec((None, 1, tcp), grp(0, 0)),
            pl.BlockSpec((spb * lanes, spb * lanes), lambda cb, gi: (0, 0)),
            pl.BlockSpec((spb * lanes, spb * lanes), lambda cb, gi: (0, 0)),
        ],
        out_specs=pl.BlockSpec((t, lanes), lambda cb, gi: (0, cb)),
        out_shape=jax.ShapeDtypeStruct((t, width), F32),
        scratch_shapes=[pltpu.VMEM((tcp, tcp), BF16), pltpu.VMEM((gpb, rows, tcp), BF16),
                        pltpu.VMEM((gpb, rows, tcp), BF16)],
        compiler_params=_params(("parallel", "arbitrary")),
        name="s5_gelu",
    )(u, acol, arow, bt, bx, ct, dtile, perm, permt)


GATHER_UNROLL = 8


def _moe_gate_up_kernel(te_ref, tv_ref, tok_ref, x_hbm, wg_ref, wu_ref, o_ref, xbuf, xb, sem, *,
                        tm):
    i = pl.program_id(0)
    j = pl.program_id(1)
    n_tiles = pl.num_programs(0)
    slot = i % 2
    valid = tv_ref[i] != 0

    def issue_tile(tile, dst_slot):
        def issue(r, carry):
            pltpu.make_async_copy(x_hbm.at[pl.ds(tok_ref[tile * tm + r], 1), :],
                                  xbuf.at[dst_slot, pl.ds(r, 1), :], sem.at[dst_slot]).start()
            return carry
        lax.fori_loop(0, tm, issue, 0, unroll=GATHER_UNROLL)

    def wait_tile(dst_slot):
        pltpu.make_async_copy(x_hbm.at[pl.ds(0, tm), :], xbuf.at[dst_slot],
                              sem.at[dst_slot]).wait()

    @pl.when(jnp.logical_and(j == 0, i == 0))
    def _():
        issue_tile(0, 0)

    @pl.when(jnp.logical_and(j == 0, valid))
    def _():
        wait_tile(slot)

    nxt = jnp.minimum(i + 1, n_tiles - 1)

    @pl.when(jnp.logical_and(j == 0, jnp.logical_and(i + 1 < n_tiles, tv_ref[nxt] != 0)))
    def _():
        issue_tile(i + 1, 1 - slot)

    @pl.when(jnp.logical_and(j == 0, valid))
    def _():
        xb[...] = xbuf[slot].astype(BF16)

    @pl.when(valid)
    def _():
        a = xb[...]
        gate = jnp.dot(a, wg_ref[...], preferred_element_type=F32)
        up = jnp.dot(a, wu_ref[...], preferred_element_type=F32)
        o_ref[...] = _epi_swiglu([gate, up], [], []).astype(o_ref.dtype)

    @pl.when(jnp.logical_not(valid))
    def _():
        o_ref[...] = jnp.zeros_like(o_ref)


def _moe_gate_up(x, w_gate, w_up, row_token, tile_expert, tile_valid, *, tm, tn):
    k = x.shape[1]
    n_rows = row_token.shape[0]
    n = w_gate.shape[2]
    tn = _pick(n, tn, V7X_LANES)
    return pl.pallas_call(
        functools.partial(_moe_gate_up_kernel, tm=tm),
        grid_spec=pltpu.PrefetchScalarGridSpec(
            num_scalar_prefetch=3,
            grid=(n_rows // tm, n // tn),
            in_specs=[pl.BlockSpec(memory_space=pl.ANY),
                      pl.BlockSpec((None, k, tn), lambda i, j, te, tv, tok: (te[i], 0, j)),
                      pl.BlockSpec((None, k, tn), lambda i, j, te, tv, tok: (te[i], 0, j))],
            out_specs=pl.BlockSpec((tm, tn), lambda i, j, te, tv, tok: (i, j)),
            scratch_shapes=[pltpu.VMEM((2, tm, k), x.dtype), pltpu.VMEM((tm, k), BF16),
                            pltpu.SemaphoreType.DMA((2,))],
        ),
        out_shape=jax.ShapeDtypeStruct((n_rows, n), BF16),
        compiler_params=_params(("arbitrary", "arbitrary")),
        name="moe_gate_up",
    )(tile_expert, tile_valid, row_token, x, w_gate, w_up)


def _grouped_kernel(te_ref, tv_ref, *refs, n_rhs, n_row, epilogue):
    lhs_ref = refs[0]
    rhs_refs = refs[1:1 + n_rhs]
    row_refs = refs[1 + n_rhs:1 + n_rhs + n_row]
    o_ref = refs[-1]

    valid = tv_ref[pl.program_id(0)] != 0

    @pl.when(valid)
    def _():
        a = lhs_ref[...]
        accs = [jnp.dot(a, r[...], preferred_element_type=F32) for r in rhs_refs]
        o_ref[...] = epilogue(accs, [r[...] for r in row_refs], []).astype(o_ref.dtype)

    @pl.when(jnp.logical_not(valid))
    def _():
        o_ref[...] = jnp.zeros_like(o_ref)


def _grouped_matmul(lhs, rhss, tile_expert, tile_valid, epilogue, out_dtype, *, tm, tn,
                    row_extras=(), name):
    r, k = lhs.shape
    n = rhss[0].shape[2]
    tn = _pick(n, tn, V7X_LANES)
    in_specs = [pl.BlockSpec((tm, k), lambda i, j, te, tv: (i, 0))]
    in_specs += [pl.BlockSpec((None, k, tn), lambda i, j, te, tv: (te[i], 0, j)) for _ in rhss]
    in_specs += [pl.BlockSpec((tm, 1), lambda i, j, te, tv: (i, 0)) for _ in row_extras]
    return pl.pallas_call(
        functools.partial(_grouped_kernel, n_rhs=len(rhss), n_row=len(row_extras),
                          epilogue=epilogue),
        grid_spec=pltpu.PrefetchScalarGridSpec(
            num_scalar_prefetch=2,
            grid=(r // tm, n // tn),
            in_specs=in_specs,
            out_specs=pl.BlockSpec((tm, tn), lambda i, j, te, tv: (i, j)),
        ),
        out_shape=jax.ShapeDtypeStruct((r, n), out_dtype),
        compiler_params=_params(("parallel", "arbitrary")),
        name=name,
    )(tile_expert, tile_valid, lhs, *rhss, *row_extras)


def _epi_row_scale(accs, rows, cols):
    return accs[0] * rows[0]


def _combine_ln_kernel(pos_ref, y_hbm, x_ref, g_ref, b_ref, of_ref, ob_ref, buf, sem, *,
                       tc, n_tokens, alpha):
    i = pl.program_id(0)
    slot = i % 2

    def issue_tile(tile, dst_slot):
        base = tile * tc

        def issue(r, carry):
            for kk in range(TOP_K):
                pltpu.make_async_copy(y_hbm.at[pl.ds(pos_ref[kk * n_tokens + base + r], 1), :],
                                      buf.at[dst_slot, kk, pl.ds(r, 1), :],
                                      sem.at[dst_slot]).start()
            return carry
        lax.fori_loop(0, tc, issue, 0, unroll=GATHER_UNROLL)

    @pl.when(i == 0)
    def _():
        issue_tile(0, 0)

    for kk in range(TOP_K):
        pltpu.make_async_copy(y_hbm.at[pl.ds(0, tc), :], buf.at[slot, kk], sem.at[slot]).wait()

    @pl.when(i + 1 < pl.num_programs(0))
    def _():
        issue_tile(i + 1, 1 - slot)

    f = buf[slot, 0]
    for kk in range(1, TOP_K):
        f = f + buf[slot, kk]
    y = _layer_norm_rows(alpha * x_ref[...] + f, g_ref[...], b_ref[...])
    of_ref[...] = y
    ob_ref[...] = y.astype(BF16)


def _combine_ln(y_sorted, pos, x, g, b, *, alpha, tc):
    t, dm = x.shape
    tc = _pick(t, tc, V7X_SUBLANES * 2)
    return pl.pallas_call(
        functools.partial(_combine_ln_kernel, tc=tc, n_tokens=t, alpha=alpha),
        grid_spec=pltpu.PrefetchScalarGridSpec(
            num_scalar_prefetch=1,
            grid=(t // tc,),
            in_specs=[pl.BlockSpec(memory_space=pl.ANY),
                      pl.BlockSpec((tc, dm), lambda i, pos: (i, 0)),
                      pl.BlockSpec((1, dm), lambda i, pos: (0, 0)),
                      pl.BlockSpec((1, dm), lambda i, pos: (0, 0))],
            out_specs=[pl.BlockSpec((tc, dm), lambda i, pos: (i, 0)),
                       pl.BlockSpec((tc, dm), lambda i, pos: (i, 0))],
            scratch_shapes=[pltpu.VMEM((2, TOP_K, tc, dm), y_sorted.dtype),
                            pltpu.SemaphoreType.DMA((2,))],
        ),
        out_shape=[jax.ShapeDtypeStruct((t, dm), F32), jax.ShapeDtypeStruct((t, dm), BF16)],
        compiler_params=_params(("arbitrary",)),
        name="moe_combine_ln",
    )(pos, y_sorted, x, g.reshape(1, dm), b.reshape(1, dm))


def _routing_tables(top_idx, top_w, n_experts, tm):
    t = top_idx.shape[0]
    n_pairs = TOP_K * t
    n_rows = n_pairs + n_experts * tm
    n_tiles = n_rows // tm
    e_flat = top_idx.T.reshape(n_pairs)
    w_flat = top_w.T.reshape(n_pairs)
    onehot = (e_flat[:, None] == jnp.arange(n_experts, dtype=jnp.int32)[None, :]).astype(jnp.int32)
    rank = jnp.sum((jnp.cumsum(onehot, axis=0) - onehot) * onehot, axis=1)
    counts = jnp.sum(onehot, axis=0)
    tiles_per = (counts + tm - 1) // tm
    tile_end = jnp.cumsum(tiles_per)
    row_start = (tile_end - tiles_per) * tm
    pos = (row_start[e_flat] + rank).astype(jnp.int32)
    row_pair = jnp.full((n_rows,), -1, jnp.int32).at[pos].set(
        jnp.arange(n_pairs, dtype=jnp.int32))
    has_pair = row_pair >= 0
    row_token = jnp.where(has_pair, row_pair % t, 0)
    row_gate = jnp.where(has_pair, w_flat[jnp.maximum(row_pair, 0)], 0.0)
    tile_ids = jnp.arange(n_tiles, dtype=jnp.int32)
    total = tile_end[-1]
    tile_valid = (tile_ids < total).astype(jnp.int32)
    clipped = jnp.minimum(tile_ids, total - 1)
    tile_expert = jnp.minimum(
        jnp.sum((tile_end[None, :] <= clipped[:, None]).astype(jnp.int32), axis=1),
        n_experts - 1).astype(jnp.int32)
    return pos, row_token, row_gate.reshape(n_rows, 1), tile_expert, tile_valid


def kernel(x, w_in, w_out, attn_lambda_q1, attn_lambda_k1, attn_lambda_q2, attn_lambda_k2, attn_subln_w, ssm_a_re, ssm_a_im, ssm_log_dt, ssm_b_re, ssm_b_im, ssm_c_re, ssm_c_im, ssm_d, ssm_w_glu, ln1_g, ln1_b, ln2_g, ln2_b, ffn_w_gate, ffn_w_up, ffn_w_down, moe_w_router, moe_w_gate, moe_w_up, moe_w_down):
    bsz, seq, dm = x.shape
    depth = w_in.shape[0]
    d = attn_lambda_q1.shape[-1]
    ssm_width = ssm_d.shape[-1]
    attn_width = w_out.shape[1] - ssm_width
    n_heads = attn_width // (2 * d)
    n_experts = moe_w_router.shape[-1]
    t = bsz * seq
    alpha = (2.0 * depth) ** 0.25
    moe_tm = _pick(t, 512, V7X_SUBLANES * 2)

    xf = x.reshape(t, dm).astype(F32)
    xb = xf.astype(BF16)
    col_scale = jnp.concatenate([jnp.full((1, attn_width), d ** -0.5 * math.log2(math.e), F32),
                                 jnp.ones((1, 2 * attn_width), F32)], axis=1)

    for l in range(depth):
        lam_init = 0.8 - 0.6 * math.exp(-0.3 * l)
        w_in_l = w_in[l].astype(BF16)
        qkv_width = 3 * attn_width
        h = _matmul(xb, [w_in_l], _epi_scale_cols, BF16, tm=1024, tn=512,
                    col_extras=[col_scale], n_out=qkv_width, name=f"w_in_qkv_{l}")
        h = h.reshape(bsz, seq, qkv_width)
        u = _matmul(xb, [w_in_l], _epi_first, F32, tm=1024, tn=512, rhs_first_col=qkv_width,
                    name=f"w_in_u_{l}")
        lam_params = jnp.stack([attn_lambda_q1[l], attn_lambda_k1[l],
                                attn_lambda_q2[l], attn_lambda_k2[l]]).astype(F32)
        o = _diff_attention(h, lam_params, attn_subln_w[l].astype(F32), n_heads=n_heads, d=d,
                            lam_init=lam_init, tq=1024, tk=1024, tk_diag=512)
        g = _s5_gelu(u, ssm_a_re[l], ssm_a_im[l], ssm_log_dt[l], ssm_b_re[l], ssm_b_im[l],
                     ssm_c_re[l], ssm_c_im[l], ssm_d[l], bsz=bsz)
        y = _matmul(g, [ssm_w_glu[l].astype(BF16)], _epi_glu, BF16, tm=1024, tn=512,
                    row_extras=[g], name=f"ssm_glu_{l}")
        is_moe = l % 2 == 1
        mixed = _matmul_ln([o.reshape(t, attn_width), y], w_out[l].astype(BF16), xf,
                           ln1_g[l].astype(F32), ln1_b[l].astype(F32), alpha=alpha,
                           tm=512, tk=512, name=f"w_out_ln_{l}",
                           w_router=moe_w_router[l // 2] if is_moe else None)
        xf, xb = mixed[0], mixed[1]
        if not is_moe:
            e = l // 2
            d_ff = ffn_w_gate.shape[-1]
            pad = (-d_ff) % FFN_K_TILE
            w_down = jnp.pad(ffn_w_down[e].astype(BF16), ((0, pad), (0, 0)))
            hid = _matmul(xb, [ffn_w_gate[e].astype(BF16), ffn_w_up[e].astype(BF16)],
                          _epi_swiglu, BF16, tm=1024, tn=256, n_out=d_ff + pad,
                          name=f"ffn_gate_up_{l}")
            xf, xb = _matmul_ln([hid], w_down, xf, ln2_g[l].astype(F32),
                                ln2_b[l].astype(F32), alpha=alpha, tm=512, tk=FFN_K_TILE,
                                name=f"ffn_down_ln_{l}")
        else:
            e = l // 2
            top_idx, top_w = mixed[2], mixed[3]
            pos, row_token, row_gate, tile_expert, tile_valid = _routing_tables(
                top_idx, top_w, n_experts, moe_tm)
            hid = _moe_gate_up(xf, moe_w_gate[e].astype(BF16), moe_w_up[e].astype(BF16),
                               row_token, tile_expert, tile_valid, tm=moe_tm, tn=512)
            ys = _grouped_matmul(hid, [moe_w_down[e].astype(BF16)], tile_expert, tile_valid,
                                 _epi_row_scale, F32, tm=moe_tm, tn=1024,
                                 row_extras=[row_gate], name=f"moe_down_{l}")
            xf, xb = _combine_ln(ys, pos, xf, ln2_g[l].astype(F32), ln2_b[l].astype(F32),
                                 alpha=alpha, tc=256)
    return xf.reshape(bsz, seq, dm).astype(x.dtype)
```

```python
import functools
import math

import jax
import jax.numpy as jnp
from jax import lax
from jax.experimental import pallas as pl
from jax.experimental.pallas import tpu as pltpu

F32 = jnp.float32
BF16 = jnp.bfloat16

V7X_LANES = 128
V7X_SUBLANES = 8
V7X_VMEM_LIMIT_BYTES = 56 * 1024 * 1024

ATTN_CHUNK = 64
TOP_K = 2
LN_EPS = 1e-5
RMS_EPS = 1e-5
MASK_VALUE = -1e30
S5_CHUNK = 64
FFN_K_TILE = 512


def _pick(dim, target, align):
    if dim <= target:
        return dim
    t = (target // align) * align
    while t >= align:
        if dim % t == 0:
            return t
        t -= align
    raise ValueError(f"no tile for dim={dim} target={target} align={align}")


def _params(semantics):
    return pltpu.CompilerParams(dimension_semantics=semantics,
                                vmem_limit_bytes=V7X_VMEM_LIMIT_BYTES)


def _mm_kernel(*refs, n_rhs, n_row, n_col, epilogue, zero_from):
    lhs_ref = refs[0]
    rhs_refs = refs[1:1 + n_rhs]
    row_refs = refs[1 + n_rhs:1 + n_rhs + n_row]
    col_refs = refs[1 + n_rhs + n_row:1 + n_rhs + n_row + n_col]
    o_ref = refs[-1]

    def compute():
        a = lhs_ref[...].astype(BF16)
        accs = [jnp.dot(a, r[...], preferred_element_type=F32) for r in rhs_refs]
        out = epilogue(accs, [r[...] for r in row_refs], [c[...] for c in col_refs])
        o_ref[...] = out.astype(o_ref.dtype)

    if zero_from is None:
        compute()
    else:
        pl.when(pl.program_id(1) < zero_from)(compute)

        @pl.when(pl.program_id(1) >= zero_from)
        def _():
            o_ref[...] = jnp.zeros_like(o_ref)


def _matmul(lhs, rhss, epilogue, out_dtype, *, tm, tn, row_extras=(), col_extras=(), name,
            n_out=None, rhs_first_col=0):
    m, k = lhs.shape
    rhs_cols = rhss[0].shape[1] - rhs_first_col
    n = rhs_cols if n_out is None else n_out
    tm = _pick(m, tm, V7X_SUBLANES * 2)
    tn = _pick(math.gcd(n, rhs_cols, rhs_first_col) if rhs_first_col else math.gcd(n, rhs_cols),
               tn, V7X_LANES)
    first = rhs_first_col // tn
    valid = min(rhs_cols, n) // tn
    zero_from = valid if n > rhs_cols else None
    in_specs = [pl.BlockSpec((tm, k), lambda i, j: (i, 0))]
    in_specs += [pl.BlockSpec((k, tn), lambda i, j: (0, first + jnp.minimum(j, valid - 1)))
                 for _ in rhss]
    in_specs += [pl.BlockSpec((tm, tn), lambda i, j: (i, j)) for _ in row_extras]
    in_specs += [pl.BlockSpec((1, tn), lambda i, j: (0, j)) for _ in col_extras]
    body = functools.partial(_mm_kernel, n_rhs=len(rhss), n_row=len(row_extras),
                             n_col=len(col_extras), epilogue=epilogue, zero_from=zero_from)
    return pl.pallas_call(
        body,
        grid=(m // tm, n // tn),
        in_specs=in_specs,
        out_specs=pl.BlockSpec((tm, tn), lambda i, j: (i, j)),
        out_shape=jax.ShapeDtypeStruct((m, n), out_dtype),
        compiler_params=_params(("parallel", "arbitrary")),
        name=name,
    )(lhs, *rhss, *row_extras, *col_extras)


def _epi_scale_cols(accs, rows, cols):
    return accs[0] * cols[0]


def _epi_first(accs, rows, cols):
    return accs[0]


def _epi_swiglu(accs, rows, cols):
    g = accs[0]
    return g * jax.nn.sigmoid(g) * accs[1]


def _epi_glu(accs, rows, cols):
    g = rows[0].astype(F32)
    return g * jax.nn.sigmoid(accs[0])


def _layer_norm_rows(r, g, b):
    mu = jnp.mean(r, axis=-1, keepdims=True)
    c = r - mu
    var = jnp.mean(c * c, axis=-1, keepdims=True)
    return c * lax.rsqrt(var + LN_EPS) * g + b


def _top2_gates(x, w_pad, n_experts):
    logits = jnp.dot(x, w_pad, preferred_element_type=F32, precision=lax.Precision.HIGHEST)
    col = lax.broadcasted_iota(jnp.int32, logits.shape, 1)
    big = jnp.int32(logits.shape[1])
    lg = jnp.where(col < n_experts, logits, -jnp.inf)
    m1 = jnp.max(lg, axis=-1, keepdims=True)
    i1 = jnp.min(jnp.where(lg == m1, col, big), axis=-1, keepdims=True)
    lg2 = jnp.where(col == i1, -jnp.inf, lg)
    m2 = jnp.max(lg2, axis=-1, keepdims=True)
    i2 = jnp.min(jnp.where(lg2 == m2, col, big), axis=-1, keepdims=True)
    e = jnp.exp(m2 - m1)
    w1 = 1.0 / (1.0 + e)
    w2 = e / (1.0 + e)
    idx = jnp.where(col == 0, i1, jnp.where(col == 1, i2, 0))
    gate = jnp.where(col == 0, w1, jnp.where(col == 1, w2, 0.0))
    return idx, gate


def _mm_ln_kernel(*refs, alpha, part_blocks, n_experts):
    n_parts = len(part_blocks)
    lhs_refs = refs[:n_parts]
    rhs_ref, x_hbm, g_ref, b_ref = refs[n_parts:n_parts + 4]
    if n_experts:
        wr_ref, of_ref, ob_ref, idx_ref, gate_ref, xbuf, xsem = refs[n_parts + 4:]
    else:
        of_ref, ob_ref, xbuf, xsem = refs[n_parts + 4:]
    kk = pl.program_id(1)
    tm = xbuf.shape[0]
    x_copy = pltpu.make_async_copy(x_hbm.at[pl.ds(pl.program_id(0) * tm, tm), :], xbuf, xsem)

    @pl.when(kk == 0)
    def _():
        x_copy.start()

    @pl.when(kk == 0)
    def _():
        of_ref[...] = jnp.dot(lhs_refs[0][...], rhs_ref[...], preferred_element_type=F32)

    first = 0
    for lhs_ref, blocks in zip(lhs_refs, part_blocks):
        @pl.when(jnp.logical_and(kk >= max(first, 1), kk < first + blocks))
        def _(lhs_ref=lhs_ref):
            of_ref[...] += jnp.dot(lhs_ref[...], rhs_ref[...], preferred_element_type=F32)
        first += blocks

    @pl.when(kk == pl.num_programs(1) - 1)
    def _():
        x_copy.wait()
        y = _layer_norm_rows(alpha * xbuf[...] + of_ref[...], g_ref[...], b_ref[...])
        of_ref[...] = y
        ob_ref[...] = y.astype(BF16)
        if n_experts:
            idx_ref[...], gate_ref[...] = _top2_gates(y, wr_ref[...], n_experts)


def _matmul_ln(lhs_parts, rhs, x, g, b, *, alpha, tm, tk, name, w_router=None):
    m = lhs_parts[0].shape[0]
    n = rhs.shape[1]
    tm = _pick(m, tm, V7X_SUBLANES * 2)
    tk = _pick(math.gcd(*[p.shape[1] for p in lhs_parts]), tk, V7X_LANES)
    part_blocks = tuple(p.shape[1] // tk for p in lhs_parts)

    def part_spec(first, blocks):
        return pl.BlockSpec((tm, tk),
                            lambda i, kk: (i, jnp.clip(kk - first, 0, blocks - 1)))

    def row_block(width):
        return pl.BlockSpec((tm, width), lambda i, kk: (i, 0))

    firsts = [sum(part_blocks[:p]) for p in range(len(part_blocks))]
    in_specs = [part_spec(f, nb) for f, nb in zip(firsts, part_blocks)] + [
        pl.BlockSpec((tk, n), lambda i, kk: (kk, 0)),
        pl.BlockSpec(memory_space=pl.ANY),
        pl.BlockSpec((1, n), lambda i, kk: (0, 0)),
        pl.BlockSpec((1, n), lambda i, kk: (0, 0)),
    ]
    operands = [*lhs_parts, rhs, x, g.reshape(1, n), b.reshape(1, n)]
    out_specs = [row_block(n), row_block(n)]
    out_shape = [jax.ShapeDtypeStruct((m, n), F32), jax.ShapeDtypeStruct((m, n), BF16)]
    n_experts = 0
    if w_router is not None:
        n_experts = w_router.shape[1]
        w_pad = jnp.zeros((n, V7X_LANES), F32).at[:, :n_experts].set(w_router.astype(F32))
        in_specs.append(pl.BlockSpec((n, V7X_LANES), lambda i, kk: (0, 0)))
        operands.append(w_pad)
        out_specs += [row_block(V7X_LANES), row_block(V7X_LANES)]
        out_shape += [jax.ShapeDtypeStruct((m, V7X_LANES), jnp.int32),
                      jax.ShapeDtypeStruct((m, V7X_LANES), F32)]
    outs = pl.pallas_call(
        functools.partial(_mm_ln_kernel, alpha=alpha, part_blocks=part_blocks,
                          n_experts=n_experts),
        grid=(m // tm, sum(part_blocks)),
        in_specs=in_specs,
        out_specs=out_specs,
        out_shape=out_shape,
        scratch_shapes=[pltpu.VMEM((tm, n), F32), pltpu.SemaphoreType.DMA(())],
        compiler_params=_params(("parallel", "arbitrary")),
        name=name,
    )(*operands)
    if w_router is None:
        return outs
    return outs[0], outs[1], outs[2][:, :TOP_K], outs[3][:, :TOP_K]


def _mm_ln_cols_kernel(*refs, alpha, n_parts, tn, n_experts):
    lhs_refs = refs[:n_parts]
    rhs_ref, x_hbm, g_ref, b_ref = refs[n_parts:n_parts + 4]
    if n_experts:
        wr_ref, of_ref, ob_ref, idx_ref, gate_ref, xbuf, xsem = refs[n_parts + 4:]
    else:
        of_ref, ob_ref, xbuf, xsem = refs[n_parts + 4:]
    j = pl.program_id(1)
    tm = xbuf.shape[0]
    x_copy = pltpu.make_async_copy(x_hbm.at[pl.ds(pl.program_id(0) * tm, tm), :], xbuf, xsem)

    @pl.when(j == 0)
    def _():
        x_copy.start()

    acc = None
    row = 0
    for lhs_ref in lhs_refs:
        depth = lhs_ref.shape[1]
        part = jnp.dot(lhs_ref[...], rhs_ref[row:row + depth, :], preferred_element_type=F32)
        acc = part if acc is None else acc + part
        row += depth
    of_ref[:, pl.ds(pl.multiple_of(j * tn, tn), tn)] = acc

    @pl.when(j == pl.num_programs(1) - 1)
    def _():
        x_copy.wait()
        y = _layer_norm_rows(alpha * xbuf[...] + of_ref[...], g_ref[...], b_ref[...])
        of_ref[...] = y
        ob_ref[...] = y.astype(BF16)
        if n_experts:
            idx_ref[...], gate_ref[...] = _top2_gates(y, wr_ref[...], n_experts)


def _matmul_ln_cols(lhs_parts, rhs, x, g, b, *, alpha, tm, tn, name, w_router=None):
    m = lhs_parts[0].shape[0]
    k, n = rhs.shape
    tm = _pick(m, tm, V7X_SUBLANES * 2)
    tn = _pick(n, tn, V7X_LANES)

    def row_block(width):
        return pl.BlockSpec((tm, width), lambda i, j: (i, 0))

    in_specs = [row_block(p.shape[1]) for p in lhs_parts] + [
        pl.BlockSpec((k, tn), lambda i, j: (0, j)),
        pl.BlockSpec(memory_space=pl.ANY),
        pl.BlockSpec((1, n), lambda i, j: (0, 0)),
        pl.BlockSpec((1, n), lambda i, j: (0, 0)),
    ]
    operands = [*lhs_parts, rhs, x, g.reshape(1, n), b.reshape(1, n)]
    out_specs = [row_block(n), row_block(n)]
    out_shape = [jax.ShapeDtypeStruct((m, n), F32), jax.ShapeDtypeStruct((m, n), BF16)]
    n_experts = 0
    if w_router is not None:
        n_experts = w_router.shape[1]
        w_pad = jnp.zeros((n, V7X_LANES), F32).at[:, :n_experts].set(w_router.astype(F32))
        in_specs.append(pl.BlockSpec((n, V7X_LANES), lambda i, j: (0, 0)))
        operands.append(w_pad)
        out_specs += [row_block(V7X_LANES), row_block(V7X_LANES)]
        out_shape += [jax.ShapeDtypeStruct((m, V7X_LANES), jnp.int32),
                      jax.ShapeDtypeStruct((m, V7X_LANES), F32)]
    outs = pl.pallas_call(
        functools.partial(_mm_ln_cols_kernel, alpha=alpha, n_parts=len(lhs_parts), tn=tn,
                          n_experts=n_experts),
        grid=(m // tm, n // tn),
        in_specs=in_specs,
        out_specs=out_specs,
        out_shape=out_shape,
        scratch_shapes=[pltpu.VMEM((tm, n), F32), pltpu.SemaphoreType.DMA(())],
        compiler_params=_params(("parallel", "arbitrary")),
        name=name,
    )(*operands)
    if w_router is None:
        return outs
    return outs[0], outs[1], outs[2][:, :TOP_K], outs[3][:, :TOP_K]


def _attn_kernel(lam_ref, w_ref, q_ref, k_ref, v_ref, o_ref, m_sc, l_sc, acc_sc, *,
                 tq, tk, tk_diag, d, lam_init):
    qi = pl.program_id(2)
    hw = 2 * d
    lanes = V7X_LANES
    m_sc[...] = jnp.full(m_sc.shape, MASK_VALUE, F32)
    l_sc[...] = jnp.zeros_like(l_sc)
    acc_sc[...] = jnp.zeros_like(acc_sc)

    def step(start, width, diag_block):
        r0 = 0 if diag_block is None else diag_block * width
        nr = tq - r0
        start = pl.multiple_of(start, width)
        kb = k_ref[pl.ds(start, width), :]
        vb = v_ref[pl.ds(start, width), :]
        if diag_block is not None:
            row = lax.broadcasted_iota(jnp.int32, (nr, width), 0) // ATTN_CHUNK
            col = lax.broadcasted_iota(jnp.int32, (nr, width), 1) // ATTN_CHUNK
            allowed = col <= row
        scores = [lax.dot_general(q_ref[r0:tq, mp * d:(mp + 1) * d], kb[:, mp * d:(mp + 1) * d],
                                  (((1,), (1,)), ((), ())), preferred_element_type=F32)
                  for mp in range(2)]
        for mp in range(2):
            s = scores[mp]
            if diag_block is not None:
                s = jnp.where(allowed, s, MASK_VALUE)
            m_prev = m_sc[mp, r0:tq]
            m_next = jnp.maximum(m_prev, jnp.max(s, axis=-1, keepdims=True))
            alpha = jnp.exp2(m_prev - m_next)
            ps = [jnp.exp2(s[:, c * lanes:(c + 1) * lanes] - m_next) for c in range(width // lanes)]
            psum = ps[0]
            for pc in ps[1:]:
                psum = psum + pc
            l_sc[mp, r0:tq] = alpha * l_sc[mp, r0:tq] + psum
            p = jnp.concatenate(ps, axis=1).astype(BF16)
            alpha_w = jnp.concatenate([alpha] * (hw // lanes), axis=1)
            acc_sc[mp, r0:tq] = alpha_w * acc_sc[mp, r0:tq] + jnp.dot(
                p, vb, preferred_element_type=F32)
            m_sc[mp, r0:tq] = m_next

    def body(j, carry):
        step(j * tk, tk, None)
        return carry

    lax.fori_loop(0, qi * (tq // tk), body, 0)
    for c in range(tq // tk_diag):
        step(qi * tq + c * tk_diag, tk_diag, c)

    lam_p = lam_ref[...]
    lam = (jnp.exp(jnp.sum(lam_p[0:1] * lam_p[1:2], axis=-1, keepdims=True))
           - jnp.exp(jnp.sum(lam_p[2:3] * lam_p[3:4], axis=-1, keepdims=True)) + lam_init)
    l0 = jnp.sum(l_sc[0], axis=-1, keepdims=True)
    l1 = jnp.sum(l_sc[1], axis=-1, keepdims=True)
    o = acc_sc[0] / l0 - lam * (acc_sc[1] / l1)
    o = o * lax.rsqrt(jnp.mean(o * o, axis=-1, keepdims=True) + RMS_EPS)
    o_ref[...] = (o * w_ref[...] * (1.0 - lam_init)).astype(o_ref.dtype)


def _diff_attention(h, lam_params, subln_w, *, n_heads, d, lam_init, tq, tk, tk_diag):
    bsz, seq, _ = h.shape
    hw = 2 * d
    tq = _pick(seq, tq, V7X_LANES)
    tk = _pick(tq, tk, V7X_LANES)
    tk_diag = _pick(tq, tk_diag, V7X_LANES)
    return pl.pallas_call(
        functools.partial(_attn_kernel, tq=tq, tk=tk, tk_diag=tk_diag, d=d, lam_init=lam_init),
        grid=(bsz, n_heads, seq // tq),
        in_specs=[
            pl.BlockSpec((4, d), lambda b, hd, i: (0, 0)),
            pl.BlockSpec((1, hw), lambda b, hd, i: (0, 0)),
            pl.BlockSpec((None, tq, hw), lambda b, hd, i: (b, i, hd)),
            pl.BlockSpec((None, seq, hw), lambda b, hd, i: (b, 0, n_heads + hd)),
            pl.BlockSpec((None, seq, hw), lambda b, hd, i: (b, 0, 2 * n_heads + hd)),
        ],
        out_specs=pl.BlockSpec((None, tq, hw), lambda b, hd, i: (b, i, hd)),
        out_shape=jax.ShapeDtypeStruct((bsz, seq, n_heads * hw), BF16),
        scratch_shapes=[pltpu.VMEM((2, tq, V7X_LANES), F32), pltpu.VMEM((2, tq, V7X_LANES), F32),
                        pltpu.VMEM((2, tq, hw), F32)],
        compiler_params=_params(("parallel", "parallel", "arbitrary")),
        name="diff_attention",
    )(lam_params, subln_w.reshape(1, hw), h, h, h)


def _s5_kernel(u_ref, acol_ref, arow_ref, bt_ref, bx_ref, ct_ref, d_ref, perm_ref, permt_ref,
               o_ref, m_ref, u_all, y_all, *, tc, p, n, n_chunks, rows):
    tcp = tc * p
    lanes = V7X_LANES
    spb = lanes // p
    gpb = lanes // p
    n_blk = tcp // lanes
    gi = pl.program_id(1)

    @pl.when(gi == 0)
    def _():
        for k in range(n_blk):
            zk = jnp.concatenate(
                [u_ref[pl.ds(k * spb + s, rows, stride=tc), :] for s in range(spb)], axis=1)
            ok = jnp.dot(zk.astype(BF16), perm_ref[...], preferred_element_type=F32)
            for g2 in range(gpb):
                u_all[g2, :, k * lanes:(k + 1) * lanes] = (
                    ok[:, g2 * lanes:(g2 + 1) * lanes].astype(BF16))

    u = u_all[gi]

    ar = acol_ref[:, 0:1]
    ai = acol_ref[:, 1:2]
    dt = acol_ref[:, 2:3]
    lane_id = lax.broadcasted_iota(jnp.int32, (1, lanes), 1)
    tau0 = (lane_id // p).astype(F32)

    def powers(t):
        mag = jnp.exp((ar * dt) * t)
        ang = (ai * dt) * t
        return mag * jnp.cos(ang), mag * jnp.sin(ang)

    def cmul2(a_re, a_im, b_re, b_im):
        return a_re * b_re - a_im * b_im, a_re * b_im + a_im * b_re

    mag1 = jnp.exp(ar * dt)
    lr = mag1 * jnp.cos(ai * dt)
    li = mag1 * jnp.sin(ai * dt)
    den = ar * ar + ai * ai
    f_re = ((lr - 1.0) * ar + li * ai) / den
    f_im = (li * ar - (lr - 1.0) * ai) / den
    bb_re, bb_im = cmul2(f_re, f_im, bt_ref[0], bt_ref[1])

    blk_re, blk_im = powers((lane_id * spb).astype(F32))
    zb_re, zb_im = cmul2(*powers((spb - 1.0) - tau0), bb_re, bb_im)
    vb_re, vb_im = cmul2(*powers(tau0), ct_ref[0], ct_ref[1])
    z_blocks, v_blocks = [], []
    for k in range(n_blk):
        kr = n_blk - 1 - k
        z_blocks.append(cmul2(zb_re, zb_im, blk_re[:, kr:kr + 1], blk_im[:, kr:kr + 1]))
        v_blocks.append(cmul2(vb_re, vb_im, blk_re[:, k:k + 1], blk_im[:, k:k + 1]))
    z_re = jnp.concatenate([z[0] for z in z_blocks], axis=1)
    z_im = jnp.concatenate([z[1] for z in z_blocks], axis=1)
    wt = jnp.concatenate([z_re, z_im], axis=0).astype(BF16)
    v_re = jnp.concatenate([v[0] for v in v_blocks], axis=1)
    v_im = jnp.concatenate([v[1] for v in v_blocks], axis=1)
    vmat = jnp.concatenate([v_re, -v_im], axis=0)

    ar2 = arow_ref[0:1, :]
    ai2 = arow_ref[1:2, :]
    dt2 = arow_ref[2:3, :]
    half = lax.broadcasted_iota(jnp.int32, (1, 2 * n), 1) < n
    sgn = jnp.where(half, -1.0, 1.0).astype(F32)
    mag1r = jnp.exp(ar2 * dt2)
    lr2 = mag1r * jnp.cos(ai2 * dt2)
    li2 = mag1r * jnp.sin(ai2 * dt2)
    den2 = ar2 * ar2 + ai2 * ai2
    fr2 = ((lr2 - 1.0) * ar2 + li2 * ai2) / den2
    fi2 = (li2 * ar2 - (lr2 - 1.0) * ai2) / den2
    bbt = fr2 * bx_ref[0] + (sgn * fi2) * bx_ref[1]
    kq = jnp.dot(bbt, vmat, preferred_element_type=F32, precision=lax.Precision.HIGHEST)

    lane = lax.broadcasted_iota(jnp.int32, (p, tcp), 1)
    m_ref[0:p, :] = kq.astype(BF16)
    for s in range(1, tc):
        blk = jnp.where(lane >= s * p, pltpu.roll(kq, s * p, axis=1), 0.0)
        m_ref[s * p:(s + 1) * p, :] = blk.astype(BF16)

    def cmul(x, r2, i2s):
        return x * r2 + pltpu.roll(x, n, axis=1) * i2s

    st = lax.dot_general(u, wt, (((1,), (1,)), ((), ())), preferred_element_type=F32)
    cidx = lax.broadcasted_iota(jnp.int32, (rows, 2 * n), 0) % n_chunks
    mag_c = jnp.exp(ar2 * dt2 * tc)
    ang_c = ai2 * dt2 * tc
    r2 = mag_c * jnp.cos(ang_c)
    i2 = sgn * mag_c * jnp.sin(ang_c)
    k = 1
    while k < n_chunks:
        sh = jnp.where(cidx >= k, pltpu.roll(st, k, axis=0), 0.0)
        st = st + cmul(sh, r2, i2)
        r2, i2 = r2 * r2 - i2 * i2, 2.0 * r2 * i2
        k *= 2
    hin = jnp.where(cidx >= 1, pltpu.roll(st, 1, axis=0), 0.0)
    hp = cmul(hin, lr2, sgn * li2)

    y = jnp.dot(u, m_ref[...], preferred_element_type=F32)
    y = y + jnp.dot(hp.astype(BF16), vmat.astype(BF16), preferred_element_type=F32)
    y = y + u.astype(F32) * d_ref[...]
    g = 0.5 * y * (1.0 + lax.erf(y * (2.0 ** -0.5)))
    y_all[gi] = g.astype(BF16)

    @pl.when(gi == gpb - 1)
    def _():
        for k in range(n_blk):
            wk = jnp.concatenate(
                [y_all[g2, :, k * lanes:(k + 1) * lanes] for g2 in range(gpb)], axis=1)
            ok = jnp.dot(wk, permt_ref[...], preferred_element_type=F32)
            for s in range(spb):
                o_ref[pl.ds(k * spb + s, rows, stride=tc), :] = ok[:, s * lanes:(s + 1) * lanes]


def _s5_gelu(u, a_re, a_im, log_dt, b_re, b_im, c_re, c_im, d_skip, *, bsz):
    t, width = u.shape
    seq = t // bsz
    n_groups, n = a_re.shape
    p = b_re.shape[-1]
    tc = S5_CHUNK
    n_chunks = seq // tc
    rows = bsz * n_chunks
    tcp = tc * p
    lanes = V7X_LANES
    gpb = lanes // p
    dt = jnp.exp(log_dt.astype(F32))
    dtb = jnp.broadcast_to(dt[:, None], (n_groups, n))
    acol = jnp.stack([a_re.astype(F32), a_im.astype(F32), dtb], axis=-1)
    arow = jnp.stack([jnp.tile(a_re.astype(F32), (1, 2)), jnp.tile(a_im.astype(F32), (1, 2)),
                      jnp.tile(dtb, (1, 2))], axis=1)
    spb = lanes // p
    bt = jnp.stack([jnp.tile(b_re.astype(F32), (1, 1, spb)),
                    jnp.tile(b_im.astype(F32), (1, 1, spb))], axis=1)
    btr = jnp.swapaxes(b_re.astype(F32), 1, 2)
    bti = jnp.swapaxes(b_im.astype(F32), 1, 2)
    bx = jnp.stack([jnp.concatenate([btr, bti], axis=-1),
                    jnp.concatenate([bti, btr], axis=-1)], axis=1)
    ctr = jnp.swapaxes(c_re.astype(F32), 1, 2)
    cti = jnp.swapaxes(c_im.astype(F32), 1, 2)
    ct = jnp.stack([jnp.tile(ctr, (1, 1, spb)), jnp.tile(cti, (1, 1, spb))], axis=1)
    dtile = jnp.tile(d_skip.astype(F32).reshape(n_groups, 1, p), (1, 1, tc))

    src = jnp.arange(spb * lanes, dtype=jnp.int32)
    dst = ((src % lanes) // p) * (spb * p) + (src // lanes) * p + src % p
    perm = (dst[:, None] == src[None, :]).astype(BF16)
    permt = perm.T

    def grp(*tail):
        return lambda cb, gi: (cb * gpb + gi,) + tail

    return pl.pallas_call(
        functools.partial(_s5_kernel, tc=tc, p=p, n=n, n_chunks=n_chunks, rows=rows),
        grid=(n_groups // gpb, gpb),
        in_specs=[
            pl.BlockSpec((t, lanes), lambda cb, gi: (0, cb)),
            pl.BlockSpec((None, n, 3), grp(0, 0)),
            pl.BlockSpec((None, 3, 2 * n), grp(0, 0)),
            pl.BlockSpec((None, 2, n, lanes), grp(0, 0, 0)),
            pl.BlockSpec((None, 2, p, 2 * n), grp(0, 0, 0)),
            pl.BlockSpec((None, 2, n, lanes), grp(0, 0, 0)),
            pl.BlockSpec((None, 1, tcp), grp(0, 0)),
            pl.BlockSpec((spb * lanes, spb * lanes), lambda cb, gi: (0, 0)),
            pl.BlockSpec((spb * lanes, spb * lanes), lambda cb, gi: (0, 0)),
        ],
        out_specs=pl.BlockSpec((t, lanes), lambda cb, gi: (0, cb)),
        out_shape=jax.ShapeDtypeStruct((t, width), F32),
        scratch_shapes=[pltpu.VMEM((tcp, tcp), BF16), pltpu.VMEM((gpb, rows, tcp), BF16),
                        pltpu.VMEM((gpb, rows, tcp), BF16)],
        compiler_params=_params(("parallel", "arbitrary")),
        name="s5_gelu",
    )(u, acol, arow, bt, bx, ct, dtile, perm, permt)


GATHER_UNROLL = 8


def _moe_gate_up_kernel(te_ref, tv_ref, tok_ref, x_hbm, wg_ref, wu_ref, o_ref, xbuf, xb, sem, *,
                        tm):
    i = pl.program_id(0)
    j = pl.program_id(1)
    n_tiles = pl.num_programs(0)
    slot = i % 2
    valid = tv_ref[i] != 0

    def issue_tile(tile, dst_slot):
        def issue(r, carry):
            pltpu.make_async_copy(x_hbm.at[pl.ds(tok_ref[tile * tm + r], 1), :],
                                  xbuf.at[dst_slot, pl.ds(r, 1), :], sem.at[dst_slot]).start()
            return carry
        lax.fori_loop(0, tm, issue, 0, unroll=GATHER_UNROLL)

    def wait_tile(dst_slot):
        pltpu.make_async_copy(x_hbm.at[pl.ds(0, tm), :], xbuf.at[dst_slot],
                              sem.at[dst_slot]).wait()

    @pl.when(jnp.logical_and(j == 0, i == 0))
    def _():
        issue_tile(0, 0)

    @pl.when(jnp.logical_and(j == 0, valid))
    def _():
        wait_tile(slot)

    nxt = jnp.minimum(i + 1, n_tiles - 1)

    @pl.when(jnp.logical_and(j == 0, jnp.logical_and(i + 1 < n_tiles, tv_ref[nxt] != 0)))
    def _():
        issue_tile(i + 1, 1 - slot)

    @pl.when(jnp.logical_and(j == 0, valid))
    def _():
        xb[...] = xbuf[slot].astype(BF16)

    @pl.when(valid)
    def _():
        a = xb[...]
        gate = jnp.dot(a, wg_ref[...], preferred_element_type=F32)
        up = jnp.dot(a, wu_ref[...], preferred_element_type=F32)
        o_ref[...] = _epi_swiglu([gate, up], [], []).astype(o_ref.dtype)

    @pl.when(jnp.logical_not(valid))
    def _():
        o_ref[...] = jnp.zeros_like(o_ref)


def _moe_gate_up(x, w_gate, w_up, row_token, tile_expert, tile_valid, *, tm, tn):
    k = x.shape[1]
    n_rows = row_token.shape[0]
    n = w_gate.shape[2]
    tn = _pick(n, tn, V7X_LANES)
    return pl.pallas_call(
        functools.partial(_moe_gate_up_kernel, tm=tm),
        grid_spec=pltpu.PrefetchScalarGridSpec(
            num_scalar_prefetch=3,
            grid=(n_rows // tm, n // tn),
            in_specs=[pl.BlockSpec(memory_space=pl.ANY),
                      pl.BlockSpec((None, k, tn), lambda i, j, te, tv, tok: (te[i], 0, j)),
                      pl.BlockSpec((None, k, tn), lambda i, j, te, tv, tok: (te[i], 0, j))],
            out_specs=pl.BlockSpec((tm, tn), lambda i, j, te, tv, tok: (i, j)),
            scratch_shapes=[pltpu.VMEM((2, tm, k), x.dtype), pltpu.VMEM((tm, k), BF16),
                            pltpu.SemaphoreType.DMA((2,))],
        ),
        out_shape=jax.ShapeDtypeStruct((n_rows, n), BF16),
        compiler_params=_params(("arbitrary", "arbitrary")),
        name="moe_gate_up",
    )(tile_expert, tile_valid, row_token, x, w_gate, w_up)


def _grouped_kernel(te_ref, tv_ref, *refs, n_rhs, n_row, epilogue):
    lhs_ref = refs[0]
    rhs_refs = refs[1:1 + n_rhs]
    row_refs = refs[1 + n_rhs:1 + n_rhs + n_row]
    o_ref = refs[-1]

    valid = tv_ref[pl.program_id(0)] != 0

    @pl.when(valid)
    def _():
        a = lhs_ref[...]
        accs = [jnp.dot(a, r[...], preferred_element_type=F32) for r in rhs_refs]
        o_ref[...] = epilogue(accs, [r[...] for r in row_refs], []).astype(o_ref.dtype)

    @pl.when(jnp.logical_not(valid))
    def _():
        o_ref[...] = jnp.zeros_like(o_ref)


def _grouped_matmul(lhs, rhss, tile_expert, tile_valid, epilogue, out_dtype, *, tm, tn,
                    row_extras=(), name):
    r, k = lhs.shape
    n = rhss[0].shape[2]
    tn = _pick(n, tn, V7X_LANES)
    in_specs = [pl.BlockSpec((tm, k), lambda i, j, te, tv: (i, 0))]
    in_specs += [pl.BlockSpec((None, k, tn), lambda i, j, te, tv: (te[i], 0, j)) for _ in rhss]
    in_specs += [pl.BlockSpec((tm, 1), lambda i, j, te, tv: (i, 0)) for _ in row_extras]
    return pl.pallas_call(
        functools.partial(_grouped_kernel, n_rhs=len(rhss), n_row=len(row_extras),
                          epilogue=epilogue),
        grid_spec=pltpu.PrefetchScalarGridSpec(
            num_scalar_prefetch=2,
            grid=(r // tm, n // tn),
            in_specs=in_specs,
            out_specs=pl.BlockSpec((tm, tn), lambda i, j, te, tv: (i, j)),
        ),
        out_shape=jax.ShapeDtypeStruct((r, n), out_dtype),
        compiler_params=_params(("parallel", "arbitrary")),
        name=name,
    )(tile_expert, tile_valid, lhs, *rhss, *row_extras)


def _epi_row_scale(accs, rows, cols):
    return accs[0] * rows[0]


def _combine_ln_kernel(pos_ref, y_hbm, x_ref, g_ref, b_ref, of_ref, ob_ref, buf, sem, *,
                       tc, n_tokens, alpha):
    i = pl.program_id(0)
    slot = i % 2

    def issue_tile(tile, dst_slot):
        base = tile * tc

        def issue(r, carry):
            for kk in range(TOP_K):
                pltpu.make_async_copy(y_hbm.at[pl.ds(pos_ref[kk * n_tokens + base + r], 1), :],
                                      buf.at[dst_slot, kk, pl.ds(r, 1), :],
                                      sem.at[dst_slot]).start()
            return carry
        lax.fori_loop(0, tc, issue, 0, unroll=GATHER_UNROLL)

    @pl.when(i == 0)
    def _():
        issue_tile(0, 0)

    for kk in range(TOP_K):
        pltpu.make_async_copy(y_hbm.at[pl.ds(0, tc), :], buf.at[slot, kk], sem.at[slot]).wait()

    @pl.when(i + 1 < pl.num_programs(0))
    def _():
        issue_tile(i + 1, 1 - slot)

    f = buf[slot, 0]
    for kk in range(1, TOP_K):
        f = f + buf[slot, kk]
    y = _layer_norm_rows(alpha * x_ref[...] + f, g_ref[...], b_ref[...])
    of_ref[...] = y
    ob_ref[...] = y.astype(BF16)


def _combine_ln(y_sorted, pos, x, g, b, *, alpha, tc):
    t, dm = x.shape
    tc = _pick(t, tc, V7X_SUBLANES * 2)
    return pl.pallas_call(
        functools.partial(_combine_ln_kernel, tc=tc, n_tokens=t, alpha=alpha),
        grid_spec=pltpu.PrefetchScalarGridSpec(
            num_scalar_prefetch=1,
            grid=(t // tc,),
            in_specs=[pl.BlockSpec(memory_space=pl.ANY),
                      pl.BlockSpec((tc, dm), lambda i, pos: (i, 0)),
                      pl.BlockSpec((1, dm), lambda i, pos: (0, 0)),
                      pl.BlockSpec((1, dm), lambda i, pos: (0, 0))],
            out_specs=[pl.BlockSpec((tc, dm), lambda i, pos: (i, 0)),
                       pl.BlockSpec((tc, dm), lambda i, pos: (i, 0))],
            scratch_shapes=[pltpu.VMEM((2, TOP_K, tc, dm), y_sorted.dtype),
                            pltpu.SemaphoreType.DMA((2,))],
        ),
        out_shape=[jax.ShapeDtypeStruct((t, dm), F32), jax.ShapeDtypeStruct((t, dm), BF16)],
        compiler_params=_params(("arbitrary",)),
        name="moe_combine_ln",
    )(pos, y_sorted, x, g.reshape(1, dm), b.reshape(1, dm))


def _routing_tables(top_idx, top_w, n_experts, tm):
    t = top_idx.shape[0]
    n_pairs = TOP_K * t
    n_rows = n_pairs + n_experts * tm
    n_tiles = n_rows // tm
    e_flat = top_idx.T.reshape(n_pairs)
    w_flat = top_w.T.reshape(n_pairs)
    onehot = (e_flat[:, None] == jnp.arange(n_experts, dtype=jnp.int32)[None, :]).astype(jnp.int32)
    rank = jnp.sum((jnp.cumsum(onehot, axis=0) - onehot) * onehot, axis=1)
    counts = jnp.sum(onehot, axis=0)
    tiles_per = (counts + tm - 1) // tm
    tile_end = jnp.cumsum(tiles_per)
    row_start = (tile_end - tiles_per) * tm
    pos = (row_start[e_flat] + rank).astype(jnp.int32)
    row_pair = jnp.full((n_rows,), -1, jnp.int32).at[pos].set(
        jnp.arange(n_pairs, dtype=jnp.int32))
    has_pair = row_pair >= 0
    row_token = jnp.where(has_pair, row_pair % t, 0)
    row_gate = jnp.where(has_pair, w_flat[jnp.maximum(row_pair, 0)], 0.0)
    tile_ids = jnp.arange(n_tiles, dtype=jnp.int32)
    total = tile_end[-1]
    tile_valid = (tile_ids < total).astype(jnp.int32)
    clipped = jnp.minimum(tile_ids, total - 1)
    tile_expert = jnp.minimum(
        jnp.sum((tile_end[None, :] <= clipped[:, None]).astype(jnp.int32), axis=1),
        n_experts - 1).astype(jnp.int32)
    return pos, row_token, row_gate.reshape(n_rows, 1), tile_expert, tile_valid


def kernel(x, w_in, w_out, attn_lambda_q1, attn_lambda_k1, attn_lambda_q2, attn_lambda_k2, attn_subln_w, ssm_a_re, ssm_a_im, ssm_log_dt, ssm_b_re, ssm_b_im, ssm_c_re, ssm_c_im, ssm_d, ssm_w_glu, ln1_g, ln1_b, ln2_g, ln2_b, ffn_w_gate, ffn_w_up, ffn_w_down, moe_w_router, moe_w_gate, moe_w_up, moe_w_down):
    bsz, seq, dm = x.shape
    depth = w_in.shape[0]
    d = attn_lambda_q1.shape[-1]
    ssm_width = ssm_d.shape[-1]
    attn_width = w_out.shape[1] - ssm_width
    n_heads = attn_width // (2 * d)
    n_experts = moe_w_router.shape[-1]
    t = bsz * seq
    alpha = (2.0 * depth) ** 0.25
    moe_tm = _pick(t, 512, V7X_SUBLANES * 2)

    xf = x.reshape(t, dm).astype(F32)
    xb = xf.astype(BF16)
    col_scale = jnp.concatenate([jnp.full((1, attn_width), d ** -0.5 * math.log2(math.e), F32),
                                 jnp.ones((1, 2 * attn_width), F32)], axis=1)

    for l in range(depth):
        lam_init = 0.8 - 0.6 * math.exp(-0.3 * l)
        w_in_l = w_in[l].astype(BF16)
        qkv_width = 3 * attn_width
        h = _matmul(xb, [w_in_l], _epi_scale_cols, BF16, tm=1024, tn=512,
                    col_extras=[col_scale], n_out=qkv_width, name=f"w_in_qkv_{l}")
        h = h.reshape(bsz, seq, qkv_width)
        u = _matmul(xb, [w_in_l], _epi_first, F32, tm=1024, tn=512, rhs_first_col=qkv_width,
                    name=f"w_in_u_{l}")
        lam_params = jnp.stack([attn_lambda_q1[l], attn_lambda_k1[l],
                                attn_lambda_q2[l], attn_lambda_k2[l]]).astype(F32)
        o = _diff_attention(h, lam_params, attn_subln_w[l].astype(F32), n_heads=n_heads, d=d,
                            lam_init=lam_init, tq=1024, tk=1024, tk_diag=512)
        g = _s5_gelu(u, ssm_a_re[l], ssm_a_im[l], ssm_log_dt[l], ssm_b_re[l], ssm_b_im[l],
                     ssm_c_re[l], ssm_c_im[l], ssm_d[l], bsz=bsz)
        y = _matmul(g, [ssm_w_glu[l].astype(BF16)], _epi_glu, BF16, tm=1024, tn=512,
                    row_extras=[g], name=f"ssm_glu_{l}")
        is_moe = l % 2 == 1
        mixed = _matmul_ln_cols([o.reshape(t, attn_width), y], w_out[l].astype(BF16), xf,
                                ln1_g[l].astype(F32), ln1_b[l].astype(F32), alpha=alpha,
                                tm=512, tn=256, name=f"w_out_ln_{l}",
                                w_router=moe_w_router[l // 2] if is_moe else None)
        xf, xb = mixed[0], mixed[1]
        if not is_moe:
            e = l // 2
            d_ff = ffn_w_gate.shape[-1]
            pad = (-d_ff) % FFN_K_TILE
            w_down = jnp.pad(ffn_w_down[e].astype(BF16), ((0, pad), (0, 0)))
            hid = _matmul(xb, [ffn_w_gate[e].astype(BF16), ffn_w_up[e].astype(BF16)],
                          _epi_swiglu, BF16, tm=1024, tn=256, n_out=d_ff + pad,
                          name=f"ffn_gate_up_{l}")
            xf, xb = _matmul_ln([hid], w_down, xf, ln2_g[l].astype(F32),
                                ln2_b[l].astype(F32), alpha=alpha, tm=512, tk=FFN_K_TILE,
                                name=f"ffn_down_ln_{l}")
        else:
            e = l // 2
            top_idx, top_w = mixed[2], mixed[3]
            pos, row_token, row_gate, tile_expert, tile_valid = _routing_tables(
                top_idx, top_w, n_experts, moe_tm)
            hid = _moe_gate_up(xf, moe_w_gate[e].astype(BF16), moe_w_up[e].astype(BF16),
                               row_token, tile_expert, tile_valid, tm=moe_tm, tn=512)
            ys = _grouped_matmul(hid, [moe_w_down[e].astype(BF16)], tile_expert, tile_valid,
                                 _epi_row_scale, F32, tm=moe_tm, tn=1024,
                                 row_extras=[row_gate], name=f"moe_down_{l}")
            xf, xb = _combine_ln(ys, pos, xf, ln2_g[l].astype(F32), ln2_b[l].astype(F32),
                                 alpha=alpha, tc=256)
    return xf.reshape(bsz, seq, dm).astype(x.dtype)
```

```python
import functools
import math

import jax
import jax.numpy as jnp
from jax import lax
from jax.experimental import pallas as pl
from jax.experimental.pallas import tpu as pltpu

F32 = jnp.float32
BF16 = jnp.bfloat16

V7X_LANES = 128
V7X_SUBLANES = 8
V7X_VMEM_LIMIT_BYTES = 56 * 1024 * 1024

ATTN_CHUNK = 64
TOP_K = 2
LN_EPS = 1e-5
RMS_EPS = 1e-5
MASK_VALUE = -1e30
S5_CHUNK = 64
FFN_K_TILE = 512


def _pick(dim, target, align):
    if dim <= target:
        return dim
    t = (target // align) * align
    while t >= align:
        if dim % t == 0:
            return t
        t -= align
    raise ValueError(f"no tile for dim={dim} target={target} align={align}")


def _params(semantics):
    return pltpu.CompilerParams(dimension_semantics=semantics,
                                vmem_limit_bytes=V7X_VMEM_LIMIT_BYTES)


def _mm_kernel(*refs, n_rhs, n_row, n_col, epilogue, zero_from):
    lhs_ref = refs[0]
    rhs_refs = refs[1:1 + n_rhs]
    row_refs = refs[1 + n_rhs:1 + n_rhs + n_row]
    col_refs = refs[1 + n_rhs + n_row:1 + n_rhs + n_row + n_col]
    o_ref = refs[-1]

    def compute():
        a = lhs_ref[...].astype(BF16)
        accs = [jnp.dot(a, r[...], preferred_element_type=F32) for r in rhs_refs]
        out = epilogue(accs, [r[...] for r in row_refs], [c[...] for c in col_refs])
        o_ref[...] = out.astype(o_ref.dtype)

    if zero_from is None:
        compute()
    else:
        pl.when(pl.program_id(1) < zero_from)(compute)

        @pl.when(pl.program_id(1) >= zero_from)
        def _():
            o_ref[...] = jnp.zeros_like(o_ref)


def _matmul(lhs, rhss, epilogue, out_dtype, *, tm, tn, row_extras=(), col_extras=(), name,
            n_out=None, rhs_first_col=0):
    m, k = lhs.shape
    rhs_cols = rhss[0].shape[1] - rhs_first_col
    n = rhs_cols if n_out is None else n_out
    tm = _pick(m, tm, V7X_SUBLANES * 2)
    tn = _pick(math.gcd(n, rhs_cols, rhs_first_col) if rhs_first_col else math.gcd(n, rhs_cols),
               tn, V7X_LANES)
    first = rhs_first_col // tn
    valid = min(rhs_cols, n) // tn
    zero_from = valid if n > rhs_cols else None
    in_specs = [pl.BlockSpec((tm, k), lambda i, j: (i, 0))]
    in_specs += [pl.BlockSpec((k, tn), lambda i, j: (0, first + jnp.minimum(j, valid - 1)))
                 for _ in rhss]
    in_specs += [pl.BlockSpec((tm, tn), lambda i, j: (i, j)) for _ in row_extras]
    in_specs += [pl.BlockSpec((1, tn), lambda i, j: (0, j)) for _ in col_extras]
    body = functools.partial(_mm_kernel, n_rhs=len(rhss), n_row=len(row_extras),
                             n_col=len(col_extras), epilogue=epilogue, zero_from=zero_from)
    return pl.pallas_call(
        body,
        grid=(m // tm, n // tn),
        in_specs=in_specs,
        out_specs=pl.BlockSpec((tm, tn), lambda i, j: (i, j)),
        out_shape=jax.ShapeDtypeStruct((m, n), out_dtype),
        compiler_params=_params(("parallel", "arbitrary")),
        name=name,
    )(lhs, *rhss, *row_extras, *col_extras)


def _epi_scale_cols(accs, rows, cols):
    return accs[0] * cols[0]


def _epi_first(accs, rows, cols):
    return accs[0]


def _epi_swiglu(accs, rows, cols):
    g = accs[0]
    return g * jax.nn.sigmoid(g) * accs[1]


def _epi_glu(accs, rows, cols):
    g = rows[0].astype(F32)
    return g * jax.nn.sigmoid(accs[0])


def _layer_norm_rows(r, g, b):
    mu = jnp.mean(r, axis=-1, keepdims=True)
    c = r - mu
    var = jnp.mean(c * c, axis=-1, keepdims=True)
    return c * lax.rsqrt(var + LN_EPS) * g + b


def _top2_gates(x, w_pad, n_experts):
    logits = jnp.dot(x, w_pad, preferred_element_type=F32, precision=lax.Precision.HIGHEST)
    col = lax.broadcasted_iota(jnp.int32, logits.shape, 1)
    big = jnp.int32(logits.shape[1])
    lg = jnp.where(col < n_experts, logits, -jnp.inf)
    m1 = jnp.max(lg, axis=-1, keepdims=True)
    i1 = jnp.min(jnp.where(lg == m1, col, big), axis=-1, keepdims=True)
    lg2 = jnp.where(col == i1, -jnp.inf, lg)
    m2 = jnp.max(lg2, axis=-1, keepdims=True)
    i2 = jnp.min(jnp.where(lg2 == m2, col, big), axis=-1, keepdims=True)
    e = jnp.exp(m2 - m1)
    w1 = 1.0 / (1.0 + e)
    w2 = e / (1.0 + e)
    idx = jnp.where(col == 0, i1, jnp.where(col == 1, i2, 0))
    gate = jnp.where(col == 0, w1, jnp.where(col == 1, w2, 0.0))
    return idx, gate


def _mm_ln_kernel(*refs, alpha, part_blocks, n_experts):
    n_parts = len(part_blocks)
    lhs_refs = refs[:n_parts]
    rhs_ref, x_hbm, g_ref, b_ref = refs[n_parts:n_parts + 4]
    if n_experts:
        wr_ref, of_ref, ob_ref, idx_ref, gate_ref, xbuf, xsem = refs[n_parts + 4:]
    else:
        of_ref, ob_ref, xbuf, xsem = refs[n_parts + 4:]
    kk = pl.program_id(1)
    tm = xbuf.shape[0]
    x_copy = pltpu.make_async_copy(x_hbm.at[pl.ds(pl.program_id(0) * tm, tm), :], xbuf, xsem)

    @pl.when(kk == 0)
    def _():
        x_copy.start()

    @pl.when(kk == 0)
    def _():
        of_ref[...] = jnp.dot(lhs_refs[0][...], rhs_ref[...], preferred_element_type=F32)

    first = 0
    for lhs_ref, blocks in zip(lhs_refs, part_blocks):
        @pl.when(jnp.logical_and(kk >= max(first, 1), kk < first + blocks))
        def _(lhs_ref=lhs_ref):
            of_ref[...] += jnp.dot(lhs_ref[...], rhs_ref[...], preferred_element_type=F32)
        first += blocks

    @pl.when(kk == pl.num_programs(1) - 1)
    def _():
        x_copy.wait()
        y = _layer_norm_rows(alpha * xbuf[...] + of_ref[...], g_ref[...], b_ref[...])
        of_ref[...] = y
        ob_ref[...] = y.astype(BF16)
        if n_experts:
            idx_ref[...], gate_ref[...] = _top2_gates(y, wr_ref[...], n_experts)


def _matmul_ln(lhs_parts, rhs, x, g, b, *, alpha, tm, tk, name, w_router=None):
    m = lhs_parts[0].shape[0]
    n = rhs.shape[1]
    tm = _pick(m, tm, V7X_SUBLANES * 2)
    tk = _pick(math.gcd(*[p.shape[1] for p in lhs_parts]), tk, V7X_LANES)
    part_blocks = tuple(p.shape[1] // tk for p in lhs_parts)

    def part_spec(first, blocks):
        return pl.BlockSpec((tm, tk),
                            lambda i, kk: (i, jnp.clip(kk - first, 0, blocks - 1)))

    def row_block(width):
        return pl.BlockSpec((tm, width), lambda i, kk: (i, 0))

    firsts = [sum(part_blocks[:p]) for p in range(len(part_blocks))]
    in_specs = [part_spec(f, nb) for f, nb in zip(firsts, part_blocks)] + [
        pl.BlockSpec((tk, n), lambda i, kk: (kk, 0)),
        pl.BlockSpec(memory_space=pl.ANY),
        pl.BlockSpec((1, n), lambda i, kk: (0, 0)),
        pl.BlockSpec((1, n), lambda i, kk: (0, 0)),
    ]
    operands = [*lhs_parts, rhs, x, g.reshape(1, n), b.reshape(1, n)]
    out_specs = [row_block(n), row_block(n)]
    out_shape = [jax.ShapeDtypeStruct((m, n), F32), jax.ShapeDtypeStruct((m, n), BF16)]
    n_experts = 0
    if w_router is not None:
        n_experts = w_router.shape[1]
        w_pad = jnp.zeros((n, V7X_LANES), F32).at[:, :n_experts].set(w_router.astype(F32))
        in_specs.append(pl.BlockSpec((n, V7X_LANES), lambda i, kk: (0, 0)))
        operands.append(w_pad)
        out_specs += [row_block(V7X_LANES), row_block(V7X_LANES)]
        out_shape += [jax.ShapeDtypeStruct((m, V7X_LANES), jnp.int32),
                      jax.ShapeDtypeStruct((m, V7X_LANES), F32)]
    outs = pl.pallas_call(
        functools.partial(_mm_ln_kernel, alpha=alpha, part_blocks=part_blocks,
                          n_experts=n_experts),
        grid=(m // tm, sum(part_blocks)),
        in_specs=in_specs,
        out_specs=out_specs,
        out_shape=out_shape,
        scratch_shapes=[pltpu.VMEM((tm, n), F32), pltpu.SemaphoreType.DMA(())],
        compiler_params=_params(("parallel", "arbitrary")),
        name=name,
    )(*operands)
    if w_router is None:
        return outs
    return outs[0], outs[1], outs[2][:, :TOP_K], outs[3][:, :TOP_K]


def _attn_kernel(lam_ref, w_ref, q_ref, k_ref, v_ref, o_ref, m_sc, l_sc, acc_sc, *,
                 tq, tk, tk_diag, d, lam_init):
    qi = pl.program_id(2)
    hw = 2 * d
    lanes = V7X_LANES
    m_sc[...] = jnp.full(m_sc.shape, MASK_VALUE, F32)
    l_sc[...] = jnp.zeros_like(l_sc)
    acc_sc[...] = jnp.zeros_like(acc_sc)

    def step(start, width, diag_block):
        r0 = 0 if diag_block is None else diag_block * width
        nr = tq - r0
        start = pl.multiple_of(start, width)
        kb = k_ref[pl.ds(start, width), :]
        vb = v_ref[pl.ds(start, width), :]
        if diag_block is not None:
            row = lax.broadcasted_iota(jnp.int32, (nr, width), 0) // ATTN_CHUNK
            col = lax.broadcasted_iota(jnp.int32, (nr, width), 1) // ATTN_CHUNK
            allowed = col <= row
        scores = [lax.dot_general(q_ref[r0:tq, mp * d:(mp + 1) * d], kb[:, mp * d:(mp + 1) * d],
                                  (((1,), (1,)), ((), ())), preferred_element_type=F32)
                  for mp in range(2)]
        for mp in range(2):
            s = scores[mp]
            if diag_block is not None:
                s = jnp.where(allowed, s, MASK_VALUE)
            m_prev = m_sc[mp, r0:tq]
            m_next = jnp.maximum(m_prev, jnp.max(s, axis=-1, keepdims=True))
            alpha = jnp.exp2(m_prev - m_next)
            ps = [jnp.exp2(s[:, c * lanes:(c + 1) * lanes] - m_next) for c in range(width // lanes)]
            psum = ps[0]
            for pc in ps[1:]:
                psum = psum + pc
            l_sc[mp, r0:tq] = alpha * l_sc[mp, r0:tq] + psum
            p = jnp.concatenate(ps, axis=1).astype(BF16)
            alpha_w = jnp.concatenate([alpha] * (hw // lanes), axis=1)
            acc_sc[mp, r0:tq] = alpha_w * acc_sc[mp, r0:tq] + jnp.dot(
                p, vb, preferred_element_type=F32)
            m_sc[mp, r0:tq] = m_next

    def body(j, carry):
        step(j * tk, tk, None)
        return carry

    lax.fori_loop(0, qi * (tq // tk), body, 0)
    for c in range(tq // tk_diag):
        step(qi * tq + c * tk_diag, tk_diag, c)

    lam_p = lam_ref[...]
    lam = (jnp.exp(jnp.sum(lam_p[0:1] * lam_p[1:2], axis=-1, keepdims=True))
           - jnp.exp(jnp.sum(lam_p[2:3] * lam_p[3:4], axis=-1, keepdims=True)) + lam_init)
    l0 = jnp.sum(l_sc[0], axis=-1, keepdims=True)
    l1 = jnp.sum(l_sc[1], axis=-1, keepdims=True)
    o = acc_sc[0] / l0 - lam * (acc_sc[1] / l1)
    o = o * lax.rsqrt(jnp.mean(o * o, axis=-1, keepdims=True) + RMS_EPS)
    o_ref[...] = (o * w_ref[...] * (1.0 - lam_init)).astype(o_ref.dtype)


def _diff_attention(h, lam_params, subln_w, *, n_heads, d, lam_init, tq, tk, tk_diag):
    bsz, seq, _ = h.shape
    hw = 2 * d
    tq = _pick(seq, tq, V7X_LANES)
    tk = _pick(tq, tk, V7X_LANES)
    tk_diag = _pick(tq, tk_diag, V7X_LANES)
    return pl.pallas_call(
        functools.partial(_attn_kernel, tq=tq, tk=tk, tk_diag=tk_diag, d=d, lam_init=lam_init),
        grid=(bsz, n_heads, seq // tq),
        in_specs=[
            pl.BlockSpec((4, d), lambda b, hd, i: (0, 0)),
            pl.BlockSpec((1, hw), lambda b, hd, i: (0, 0)),
            pl.BlockSpec((None, tq, hw), lambda b, hd, i: (b, i, hd)),
            pl.BlockSpec((None, seq, hw), lambda b, hd, i: (b, 0, n_heads + hd)),
            pl.BlockSpec((None, seq, hw), lambda b, hd, i: (b, 0, 2 * n_heads + hd)),
        ],
        out_specs=pl.BlockSpec((None, tq, hw), lambda b, hd, i: (b, i, hd)),
        out_shape=jax.ShapeDtypeStruct((bsz, seq, n_heads * hw), BF16),
        scratch_shapes=[pltpu.VMEM((2, tq, V7X_LANES), F32), pltpu.VMEM((2, tq, V7X_LANES), F32),
                        pltpu.VMEM((2, tq, hw), F32)],
        compiler_params=_params(("parallel", "parallel", "arbitrary")),
        name="diff_attention",
    )(lam_params, subln_w.reshape(1, hw), h, h, h)


def _s5_kernel(u_ref, acol_ref, arow_ref, bt_ref, bx_ref, ct_ref, d_ref, perm_ref, permt_ref,
               o_ref, m_ref, u_all, y_all, *, tc, p, n, n_chunks, rows):
    tcp = tc * p
    lanes = V7X_LANES
    spb = lanes // p
    gpb = lanes // p
    n_blk = tcp // lanes
    gi = pl.program_id(1)

    @pl.when(gi == 0)
    def _():
        for k in range(n_blk):
            zk = jnp.concatenate(
                [u_ref[pl.ds(k * spb + s, rows, stride=tc), :] for s in range(spb)], axis=1)
            ok = jnp.dot(zk.astype(BF16), perm_ref[...], preferred_element_type=F32)
            for g2 in range(gpb):
                u_all[g2, :, k * lanes:(k + 1) * lanes] = (
                    ok[:, g2 * lanes:(g2 + 1) * lanes].astype(BF16))

    u = u_all[gi]

    ar = acol_ref[:, 0:1]
    ai = acol_ref[:, 1:2]
    dt = acol_ref[:, 2:3]
    lane_id = lax.broadcasted_iota(jnp.int32, (1, lanes), 1)
    tau0 = (lane_id // p).astype(F32)

    def powers(t):
        mag = jnp.exp((ar * dt) * t)
        ang = (ai * dt) * t
        return mag * jnp.cos(ang), mag * jnp.sin(ang)

    def cmul2(a_re, a_im, b_re, b_im):
        return a_re * b_re - a_im * b_im, a_re * b_im + a_im * b_re

    mag1 = jnp.exp(ar * dt)
    lr = mag1 * jnp.cos(ai * dt)
    li = mag1 * jnp.sin(ai * dt)
    den = ar * ar + ai * ai
    f_re = ((lr - 1.0) * ar + li * ai) / den
    f_im = (li * ar - (lr - 1.0) * ai) / den
    bb_re, bb_im = cmul2(f_re, f_im, bt_ref[0], bt_ref[1])

    blk_re, blk_im = powers((lane_id * spb).astype(F32))
    zb_re, zb_im = cmul2(*powers((spb - 1.0) - tau0), bb_re, bb_im)
    vb_re, vb_im = cmul2(*powers(tau0), ct_ref[0], ct_ref[1])
    z_blocks, v_blocks = [], []
    for k in range(n_blk):
        kr = n_blk - 1 - k
        z_blocks.append(cmul2(zb_re, zb_im, blk_re[:, kr:kr + 1], blk_im[:, kr:kr + 1]))
        v_blocks.append(cmul2(vb_re, vb_im, blk_re[:, k:k + 1], blk_im[:, k:k + 1]))
    z_re = jnp.concatenate([z[0] for z in z_blocks], axis=1)
    z_im = jnp.concatenate([z[1] for z in z_blocks], axis=1)
    wt = jnp.concatenate([z_re, z_im], axis=0).astype(BF16)
    v_re = jnp.concatenate([v[0] for v in v_blocks], axis=1)
    v_im = jnp.concatenate([v[1] for v in v_blocks], axis=1)
    vmat = jnp.concatenate([v_re, -v_im], axis=0)

    ar2 = arow_ref[0:1, :]
    ai2 = arow_ref[1:2, :]
    dt2 = arow_ref[2:3, :]
    half = lax.broadcasted_iota(jnp.int32, (1, 2 * n), 1) < n
    sgn = jnp.where(half, -1.0, 1.0).astype(F32)
    mag1r = jnp.exp(ar2 * dt2)
    lr2 = mag1r * jnp.cos(ai2 * dt2)
    li2 = mag1r * jnp.sin(ai2 * dt2)
    den2 = ar2 * ar2 + ai2 * ai2
    fr2 = ((lr2 - 1.0) * ar2 + li2 * ai2) / den2
    fi2 = (li2 * ar2 - (lr2 - 1.0) * ai2) / den2
    bbt = fr2 * bx_ref[0] + (sgn * fi2) * bx_ref[1]
    kq = jnp.dot(bbt, vmat, preferred_element_type=F32, precision=lax.Precision.HIGHEST)

    lane = lax.broadcasted_iota(jnp.int32, (p, tcp), 1)
    m_ref[0:p, :] = kq.astype(BF16)
    for s in range(1, tc):
        blk = jnp.where(lane >= s * p, pltpu.roll(kq, s * p, axis=1), 0.0)
        m_ref[s * p:(s + 1) * p, :] = blk.astype(BF16)

    def cmul(x, r2, i2s):
        return x * r2 + pltpu.roll(x, n, axis=1) * i2s

    st = lax.dot_general(u, wt, (((1,), (1,)), ((), ())), preferred_element_type=F32)
    cidx = lax.broadcasted_iota(jnp.int32, (rows, 2 * n), 0) % n_chunks
    mag_c = jnp.exp(ar2 * dt2 * tc)
    ang_c = ai2 * dt2 * tc
    r2 = mag_c * jnp.cos(ang_c)
    i2 = sgn * mag_c * jnp.sin(ang_c)
    k = 1
    while k < n_chunks:
        sh = jnp.where(cidx >= k, pltpu.roll(st, k, axis=0), 0.0)
        st = st + cmul(sh, r2, i2)
        r2, i2 = r2 * r2 - i2 * i2, 2.0 * r2 * i2
        k *= 2
    hin = jnp.where(cidx >= 1, pltpu.roll(st, 1, axis=0), 0.0)
    hp = cmul(hin, lr2, sgn * li2)

    y = jnp.dot(u, m_ref[...], preferred_element_type=F32)
    y = y + jnp.dot(hp.astype(BF16), vmat.astype(BF16), preferred_element_type=F32)
    y = y + u.astype(F32) * d_ref[...]
    g = 0.5 * y * (1.0 + lax.erf(y * (2.0 ** -0.5)))
    y_all[gi] = g.astype(BF16)

    @pl.when(gi == gpb - 1)
    def _():
        for k in range(n_blk):
            wk = jnp.concatenate(
                [y_all[g2, :, k * lanes:(k + 1) * lanes] for g2 in range(gpb)], axis=1)
            ok = jnp.dot(wk, permt_ref[...], preferred_element_type=F32)
            for s in range(spb):
                o_ref[pl.ds(k * spb + s, rows, stride=tc), :] = ok[:, s * lanes:(s + 1) * lanes]


def _s5_gelu(u, a_re, a_im, log_dt, b_re, b_im, c_re, c_im, d_skip, *, bsz):
    t, width = u.shape
    seq = t // bsz
    n_groups, n = a_re.shape
    p = b_re.shape[-1]
    tc = S5_CHUNK
    n_chunks = seq // tc
    rows = bsz * n_chunks
    tcp = tc * p
    lanes = V7X_LANES
    gpb = lanes // p
    dt = jnp.exp(log_dt.astype(F32))
    dtb = jnp.broadcast_to(dt[:, None], (n_groups, n))
    acol = jnp.stack([a_re.astype(F32), a_im.astype(F32), dtb], axis=-1)
    arow = jnp.stack([jnp.tile(a_re.astype(F32), (1, 2)), jnp.tile(a_im.astype(F32), (1, 2)),
                      jnp.tile(dtb, (1, 2))], axis=1)
    spb = lanes // p
    bt = jnp.stack([jnp.tile(b_re.astype(F32), (1, 1, spb)),
                    jnp.tile(b_im.astype(F32), (1, 1, spb))], axis=1)
    btr = jnp.swapaxes(b_re.astype(F32), 1, 2)
    bti = jnp.swapaxes(b_im.astype(F32), 1, 2)
    bx = jnp.stack([jnp.concatenate([btr, bti], axis=-1),
                    jnp.concatenate([bti, btr], axis=-1)], axis=1)
    ctr = jnp.swapaxes(c_re.astype(F32), 1, 2)
    cti = jnp.swapaxes(c_im.astype(F32), 1, 2)
    ct = jnp.stack([jnp.tile(ctr, (1, 1, spb)), jnp.tile(cti, (1, 1, spb))], axis=1)
    dtile = jnp.tile(d_skip.astype(F32).reshape(n_groups, 1, p), (1, 1, tc))

    src = jnp.arange(spb * lanes, dtype=jnp.int32)
    dst = ((src % lanes) // p) * (spb * p) + (src // lanes) * p + src % p
    perm = (dst[:, None] == src[None, :]).astype(BF16)
    permt = perm.T

    def grp(*tail):
        return lambda cb, gi: (cb * gpb + gi,) + tail

    return pl.pallas_call(
        functools.partial(_s5_kernel, tc=tc, p=p, n=n, n_chunks=n_chunks, rows=rows),
        grid=(n_groups // gpb, gpb),
        in_specs=[
            pl.BlockSpec((t, lanes), lambda cb, gi: (0, cb)),
            pl.BlockSpec((None, n, 3), grp(0, 0)),
            pl.BlockSpec((None, 3, 2 * n), grp(0, 0)),
            pl.BlockSpec((None, 2, n, lanes), grp(0, 0, 0)),
            pl.BlockSpec((None, 2, p, 2 * n), grp(0, 0, 0)),
            pl.BlockSpec((None, 2, n, lanes), grp(0, 0, 0)),
            pl.BlockSpec((None, 1, tcp), grp(0, 0)),
            pl.BlockSpec((spb * lanes, spb * lanes), lambda cb, gi: (0, 0)),
            pl.BlockSpec((spb * lanes, spb * lanes), lambda cb, gi: (0, 0)),
        ],
        out_specs=pl.BlockSpec((t, lanes), lambda cb, gi: (0, cb)),
        out_shape=jax.ShapeDtypeStruct((t, width), F32),
        scratch_shapes=[pltpu.VMEM((tcp, tcp), BF16), pltpu.VMEM((gpb, rows, tcp), BF16),
                        pltpu.VMEM((gpb, rows, tcp), BF16)],
        compiler_params=_params(("parallel", "arbitrary")),
        name="s5_gelu",
    )(u, acol, arow, bt, bx, ct, dtile, perm, permt)


GATHER_UNROLL = 8


def _moe_gate_up_kernel(te_ref, tv_ref, tok_ref, x_hbm, wg_ref, wu_ref, o_ref, xbuf, xb, sem, *,
                        tm):
    i = pl.program_id(0)
    j = pl.program_id(1)
    n_tiles = pl.num_programs(0)
    slot = i % 2
    valid = tv_ref[i] != 0

    def issue_tile(tile, dst_slot):
        def issue(r, carry):
            pltpu.make_async_copy(x_hbm.at[pl.ds(tok_ref[tile * tm + r], 1), :],
                                  xbuf.at[dst_slot, pl.ds(r, 1), :], sem.at[dst_slot]).start()
            return carry
        lax.fori_loop(0, tm, issue, 0, unroll=GATHER_UNROLL)

    def wait_tile(dst_slot):
        pltpu.make_async_copy(x_hbm.at[pl.ds(0, tm), :], xbuf.at[dst_slot],
                              sem.at[dst_slot]).wait()

    @pl.when(jnp.logical_and(j == 0, i == 0))
    def _():
        issue_tile(0, 0)

    @pl.when(jnp.logical_and(j == 0, valid))
    def _():
        wait_tile(slot)

    nxt = jnp.minimum(i + 1, n_tiles - 1)

    @pl.when(jnp.logical_and(j == 0, jnp.logical_and(i + 1 < n_tiles, tv_ref[nxt] != 0)))
    def _():
        issue_tile(i + 1, 1 - slot)

    @pl.when(jnp.logical_and(j == 0, valid))
    def _():
        xb[...] = xbuf[slot].astype(BF16)

    @pl.when(valid)
    def _():
        a = xb[...]
        gate = jnp.dot(a, wg_ref[...], preferred_element_type=F32)
        up = jnp.dot(a, wu_ref[...], preferred_element_type=F32)
        o_ref[...] = _epi_swiglu([gate, up], [], []).astype(o_ref.dtype)

    @pl.when(jnp.logical_not(valid))
    def _():
        o_ref[...] = jnp.zeros_like(o_ref)


def _moe_gate_up(x, w_gate, w_up, row_token, tile_expert, tile_valid, *, tm, tn):
    k = x.shape[1]
    n_rows = row_token.shape[0]
    n = w_gate.shape[2]
    tn = _pick(n, tn, V7X_LANES)
    return pl.pallas_call(
        functools.partial(_moe_gate_up_kernel, tm=tm),
        grid_spec=pltpu.PrefetchScalarGridSpec(
            num_scalar_prefetch=3,
            grid=(n_rows // tm, n // tn),
            in_specs=[pl.BlockSpec(memory_space=pl.ANY),
                      pl.BlockSpec((None, k, tn), lambda i, j, te, tv, tok: (te[i], 0, j)),
                      pl.BlockSpec((None, k, tn), lambda i, j, te, tv, tok: (te[i], 0, j))],
            out_specs=pl.BlockSpec((tm, tn), lambda i, j, te, tv, tok: (i, j)),
            scratch_shapes=[pltpu.VMEM((2, tm, k), x.dtype), pltpu.VMEM((tm, k), BF16),
                            pltpu.SemaphoreType.DMA((2,))],
        ),
        out_shape=jax.ShapeDtypeStruct((n_rows, n), BF16),
        compiler_params=_params(("arbitrary", "arbitrary")),
        name="moe_gate_up",
    )(tile_expert, tile_valid, row_token, x, w_gate, w_up)


def _grouped_kernel(te_ref, tv_ref, *refs, n_rhs, n_row, epilogue):
    lhs_ref = refs[0]
    rhs_refs = refs[1:1 + n_rhs]
    row_refs = refs[1 + n_rhs:1 + n_rhs + n_row]
    o_ref = refs[-1]

    valid = tv_ref[pl.program_id(0)] != 0

    @pl.when(valid)
    def _():
        a = lhs_ref[...]
        accs = [jnp.dot(a, r[...], preferred_element_type=F32) for r in rhs_refs]
        o_ref[...] = epilogue(accs, [r[...] for r in row_refs], []).astype(o_ref.dtype)

    @pl.when(jnp.logical_not(valid))
    def _():
        o_ref[...] = jnp.zeros_like(o_ref)


def _grouped_matmul(lhs, rhss, tile_expert, tile_valid, epilogue, out_dtype, *, tm, tn,
                    row_extras=(), name):
    r, k = lhs.shape
    n = rhss[0].shape[2]
    tn = _pick(n, tn, V7X_LANES)
    in_specs = [pl.BlockSpec((tm, k), lambda i, j, te, tv: (i, 0))]
    in_specs += [pl.BlockSpec((None, k, tn), lambda i, j, te, tv: (te[i], 0, j)) for _ in rhss]
    in_specs += [pl.BlockSpec((tm, 1), lambda i, j, te, tv: (i, 0)) for _ in row_extras]
    return pl.pallas_call(
        functools.partial(_grouped_kernel, n_rhs=len(rhss), n_row=len(row_extras),
                          epilogue=epilogue),
        grid_spec=pltpu.PrefetchScalarGridSpec(
            num_scalar_prefetch=2,
            grid=(r // tm, n // tn),
            in_specs=in_specs,
            out_specs=pl.BlockSpec((tm, tn), lambda i, j, te, tv: (i, j)),
        ),
        out_shape=jax.ShapeDtypeStruct((r, n), out_dtype),
        compiler_params=_params(("parallel", "arbitrary")),
        name=name,
    )(tile_expert, tile_valid, lhs, *rhss, *row_extras)


def _epi_row_scale(accs, rows, cols):
    return accs[0] * rows[0]


def _combine_ln_kernel(pos_ref, y_hbm, x_ref, g_ref, b_ref, of_ref, ob_ref, buf, sem, *,
                       tc, n_tokens, alpha):
    i = pl.program_id(0)
    slot = i % 2

    def issue_tile(tile, dst_slot):
        base = tile * tc

        def issue(r, carry):
            for kk in range(TOP_K):
                pltpu.make_async_copy(y_hbm.at[pl.ds(pos_ref[kk * n_tokens + base + r], 1), :],
                                      buf.at[dst_slot, kk, pl.ds(r, 1), :],
                                      sem.at[dst_slot]).start()
            return carry
        lax.fori_loop(0, tc, issue, 0, unroll=GATHER_UNROLL)

    @pl.when(i == 0)
    def _():
        issue_tile(0, 0)

    for kk in range(TOP_K):
        pltpu.make_async_copy(y_hbm.at[pl.ds(0, tc), :], buf.at[slot, kk], sem.at[slot]).wait()

    @pl.when(i + 1 < pl.num_programs(0))
    def _():
        issue_tile(i + 1, 1 - slot)

    f = buf[slot, 0]
    for kk in range(1, TOP_K):
        f = f + buf[slot, kk]
    y = _layer_norm_rows(alpha * x_ref[...] + f, g_ref[...], b_ref[...])
    of_ref[...] = y
    ob_ref[...] = y.astype(BF16)


def _combine_ln(y_sorted, pos, x, g, b, *, alpha, tc):
    t, dm = x.shape
    tc = _pick(t, tc, V7X_SUBLANES * 2)
    return pl.pallas_call(
        functools.partial(_combine_ln_kernel, tc=tc, n_tokens=t, alpha=alpha),
        grid_spec=pltpu.PrefetchScalarGridSpec(
            num_scalar_prefetch=1,
            grid=(t // tc,),
            in_specs=[pl.BlockSpec(memory_space=pl.ANY),
                      pl.BlockSpec((tc, dm), lambda i, pos: (i, 0)),
                      pl.BlockSpec((1, dm), lambda i, pos: (0, 0)),
                      pl.BlockSpec((1, dm), lambda i, pos: (0, 0))],
            out_specs=[pl.BlockSpec((tc, dm), lambda i, pos: (i, 0)),
                       pl.BlockSpec((tc, dm), lambda i, pos: (i, 0))],
            scratch_shapes=[pltpu.VMEM((2, TOP_K, tc, dm), y_sorted.dtype),
                            pltpu.SemaphoreType.DMA((2,))],
        ),
        out_shape=[jax.ShapeDtypeStruct((t, dm), F32), jax.ShapeDtypeStruct((t, dm), BF16)],
        compiler_params=_params(("arbitrary",)),
        name="moe_combine_ln",
    )(pos, y_sorted, x, g.reshape(1, dm), b.reshape(1, dm))


def _routing_tables(top_idx, top_w, n_experts, tm):
    t = top_idx.shape[0]
    n_pairs = TOP_K * t
    n_rows = n_pairs + n_experts * tm
    n_tiles = n_rows // tm
    e_flat = top_idx.T.reshape(n_pairs)
    w_flat = top_w.T.reshape(n_pairs)
    onehot = (e_flat[:, None] == jnp.arange(n_experts, dtype=jnp.int32)[None, :]).astype(jnp.int32)
    rank = jnp.sum((jnp.cumsum(onehot, axis=0) - onehot) * onehot, axis=1)
    counts = jnp.sum(onehot, axis=0)
    tiles_per = (counts + tm - 1) // tm
    tile_end = jnp.cumsum(tiles_per)
    row_start = (tile_end - tiles_per) * tm
    pos = (row_start[e_flat] + rank).astype(jnp.int32)
    row_pair = jnp.full((n_rows,), -1, jnp.int32).at[pos].set(
        jnp.arange(n_pairs, dtype=jnp.int32))
    has_pair = row_pair >= 0
    row_token = jnp.where(has_pair, row_pair % t, 0)
    row_gate = jnp.where(has_pair, w_flat[jnp.maximum(row_pair, 0)], 0.0)
    tile_ids = jnp.arange(n_tiles, dtype=jnp.int32)
    total = tile_end[-1]
    tile_valid = (tile_ids < total).astype(jnp.int32)
    clipped = jnp.minimum(tile_ids, total - 1)
    tile_expert = jnp.minimum(
        jnp.sum((tile_end[None, :] <= clipped[:, None]).astype(jnp.int32), axis=1),
        n_experts - 1).astype(jnp.int32)
    return pos, row_token, row_gate.reshape(n_rows, 1), tile_expert, tile_valid


def kernel(x, w_in, w_out, attn_lambda_q1, attn_lambda_k1, attn_lambda_q2, attn_lambda_k2, attn_subln_w, ssm_a_re, ssm_a_im, ssm_log_dt, ssm_b_re, ssm_b_im, ssm_c_re, ssm_c_im, ssm_d, ssm_w_glu, ln1_g, ln1_b, ln2_g, ln2_b, ffn_w_gate, ffn_w_up, ffn_w_down, moe_w_router, moe_w_gate, moe_w_up, moe_w_down):
    bsz, seq, dm = x.shape
    depth = w_in.shape[0]
    d = attn_lambda_q1.shape[-1]
    ssm_width = ssm_d.shape[-1]
    attn_width = w_out.shape[1] - ssm_width
    n_heads = attn_width // (2 * d)
    n_experts = moe_w_router.shape[-1]
    t = bsz * seq
    alpha = (2.0 * depth) ** 0.25
    moe_tm = _pick(t, 512, V7X_SUBLANES * 2)

    xf = x.reshape(t, dm).astype(F32)
    xb = xf
    col_scale = jnp.concatenate([jnp.full((1, attn_width), d ** -0.5 * math.log2(math.e), F32),
                                 jnp.ones((1, 2 * attn_width), F32)], axis=1)

    for l in range(depth):
        lam_init = 0.8 - 0.6 * math.exp(-0.3 * l)
        w_in_l = w_in[l].astype(BF16)
        qkv_width = 3 * attn_width
        h = _matmul(xb, [w_in_l], _epi_scale_cols, BF16, tm=1024, tn=512,
                    col_extras=[col_scale], n_out=qkv_width, name=f"w_in_qkv_{l}")
        h = h.reshape(bsz, seq, qkv_width)
        u = _matmul(xb, [w_in_l], _epi_first, F32, tm=1024, tn=512, rhs_first_col=qkv_width,
                    name=f"w_in_u_{l}")
        lam_params = jnp.stack([attn_lambda_q1[l], attn_lambda_k1[l],
                                attn_lambda_q2[l], attn_lambda_k2[l]]).astype(F32)
        o = _diff_attention(h, lam_params, attn_subln_w[l].astype(F32), n_heads=n_heads, d=d,
                            lam_init=lam_init, tq=1024, tk=1024, tk_diag=512)
        g = _s5_gelu(u, ssm_a_re[l], ssm_a_im[l], ssm_log_dt[l], ssm_b_re[l], ssm_b_im[l],
                     ssm_c_re[l], ssm_c_im[l], ssm_d[l], bsz=bsz)
        y = _matmul(g, [ssm_w_glu[l].astype(BF16)], _epi_glu, BF16, tm=1024, tn=512,
                    row_extras=[g], name=f"ssm_glu_{l}")
        is_moe = l % 2 == 1
        mixed = _matmul_ln([o.reshape(t, attn_width), y], w_out[l].astype(BF16), xf,
                           ln1_g[l].astype(F32), ln1_b[l].astype(F32), alpha=alpha,
                           tm=512, tk=512, name=f"w_out_ln_{l}",
                           w_router=moe_w_router[l // 2] if is_moe else None)
        xf, xb = mixed[0], mixed[1]
        if not is_moe:
            e = l // 2
            d_ff = ffn_w_gate.shape[-1]
            pad = (-d_ff) % FFN_K_TILE
            w_down = jnp.pad(ffn_w_down[e].astype(BF16), ((0, pad), (0, 0)))
            hid = _matmul(xb, [ffn_w_gate[e].astype(BF16), ffn_w_up[e].astype(BF16)],
                          _epi_swiglu, BF16, tm=1024, tn=256, n_out=d_ff + pad,
                          name=f"ffn_gate_up_{l}")
            xf, xb = _matmul_ln([hid], w_down, xf, ln2_g[l].astype(F32),
                                ln2_b[l].astype(F32), alpha=alpha, tm=512, tk=FFN_K_TILE,
                                name=f"ffn_down_ln_{l}")
        else:
            e = l // 2
            top_idx, top_w = mixed[2], mixed[3]
            pos, row_token, row_gate, tile_expert, tile_valid = _routing_tables(
                top_idx, top_w, n_experts, moe_tm)
            hid = _moe_gate_up(xf, moe_w_gate[e].astype(BF16), moe_w_up[e].astype(BF16),
                               row_token, tile_expert, tile_valid, tm=moe_tm, tn=512)
            ys = _grouped_matmul(hid, [moe_w_down[e].astype(BF16)], tile_expert, tile_valid,
                                 _epi_row_scale, F32, tm=moe_tm, tn=1024,
                                 row_extras=[row_gate], name=f"moe_down_{l}")
            xf, xb = _combine_ln(ys, pos, xf, ln2_g[l].astype(F32), ln2_b[l].astype(F32),
                                 alpha=alpha, tc=256)
    return xf.reshape(bsz, seq, dm).astype(x.dtype)
```

```python
import functools
import math

import jax
import jax.numpy as jnp
from jax import lax
from jax.experimental import pallas as pl
from jax.experimental.pallas import tpu as pltpu

F32 = jnp.float32
BF16 = jnp.bfloat16

V7X_LANES = 128
V7X_SUBLANES = 8
V7X_VMEM_LIMIT_BYTES = 56 * 1024 * 1024

ATTN_CHUNK = 64
TOP_K = 2
LN_EPS = 1e-5
RMS_EPS = 1e-5
MASK_VALUE = -1e30
S5_CHUNK = 64
FFN_K_TILE = 512


def _pick(dim, target, align):
    if dim <= target:
        return dim
    t = (target // align) * align
    while t >= align:
        if dim % t == 0:
            return t
        t -= align
    raise ValueError(f"no tile for dim={dim} target={target} align={align}")


def _params(semantics):
    return pltpu.CompilerParams(dimension_semantics=semantics,
                                vmem_limit_bytes=V7X_VMEM_LIMIT_BYTES)


def _mm_kernel(*refs, n_rhs, n_row, n_col, epilogue, zero_from):
    lhs_ref = refs[0]
    rhs_refs = refs[1:1 + n_rhs]
    row_refs = refs[1 + n_rhs:1 + n_rhs + n_row]
    col_refs = refs[1 + n_rhs + n_row:1 + n_rhs + n_row + n_col]
    o_ref = refs[-1]

    def compute():
        a = lhs_ref[...].astype(BF16)
        accs = [jnp.dot(a, r[...], preferred_element_type=F32) for r in rhs_refs]
        out = epilogue(accs, [r[...] for r in row_refs], [c[...] for c in col_refs])
        o_ref[...] = out.astype(o_ref.dtype)

    if zero_from is None:
        compute()
    else:
        pl.when(pl.program_id(1) < zero_from)(compute)

        @pl.when(pl.program_id(1) >= zero_from)
        def _():
            o_ref[...] = jnp.zeros_like(o_ref)


def _matmul(lhs, rhss, epilogue, out_dtype, *, tm, tn, row_extras=(), col_extras=(), name,
            n_out=None, rhs_first_col=0):
    m, k = lhs.shape
    rhs_cols = rhss[0].shape[1] - rhs_first_col
    n = rhs_cols if n_out is None else n_out
    tm = _pick(m, tm, V7X_SUBLANES * 2)
    tn = _pick(math.gcd(n, rhs_cols, rhs_first_col) if rhs_first_col else math.gcd(n, rhs_cols),
               tn, V7X_LANES)
    first = rhs_first_col // tn
    valid = min(rhs_cols, n) // tn
    zero_from = valid if n > rhs_cols else None
    in_specs = [pl.BlockSpec((tm, k), lambda i, j: (i, 0))]
    in_specs += [pl.BlockSpec((k, tn), lambda i, j: (0, first + jnp.minimum(j, valid - 1)))
                 for _ in rhss]
    in_specs += [pl.BlockSpec((tm, tn), lambda i, j: (i, j)) for _ in row_extras]
    in_specs += [pl.BlockSpec((1, tn), lambda i, j: (0, j)) for _ in col_extras]
    body = functools.partial(_mm_kernel, n_rhs=len(rhss), n_row=len(row_extras),
                             n_col=len(col_extras), epilogue=epilogue, zero_from=zero_from)
    return pl.pallas_call(
        body,
        grid=(m // tm, n // tn),
        in_specs=in_specs,
        out_specs=pl.BlockSpec((tm, tn), lambda i, j: (i, j)),
        out_shape=jax.ShapeDtypeStruct((m, n), out_dtype),
        compiler_params=_params(("parallel", "arbitrary")),
        name=name,
    )(lhs, *rhss, *row_extras, *col_extras)


def _epi_scale_cols(accs, rows, cols):
    return accs[0] * cols[0]


def _epi_first(accs, rows, cols):
    return accs[0]


def _epi_swiglu(accs, rows, cols):
    g = accs[0]
    return g * jax.nn.sigmoid(g) * accs[1]


def _epi_glu(accs, rows, cols):
    g = rows[0].astype(F32)
    return g * jax.nn.sigmoid(accs[0])


def _layer_norm_rows(r, g, b):
    mu = jnp.mean(r, axis=-1, keepdims=True)
    c = r - mu
    var = jnp.mean(c * c, axis=-1, keepdims=True)
    return c * lax.rsqrt(var + LN_EPS) * g + b


def _top2_gates(x, w_pad, n_experts):
    logits = jnp.dot(x, w_pad, preferred_element_type=F32, precision=lax.Precision.HIGHEST)
    col = lax.broadcasted_iota(jnp.int32, logits.shape, 1)
    big = jnp.int32(logits.shape[1])
    lg = jnp.where(col < n_experts, logits, -jnp.inf)
    m1 = jnp.max(lg, axis=-1, keepdims=True)
    i1 = jnp.min(jnp.where(lg == m1, col, big), axis=-1, keepdims=True)
    lg2 = jnp.where(col == i1, -jnp.inf, lg)
    m2 = jnp.max(lg2, axis=-1, keepdims=True)
    i2 = jnp.min(jnp.where(lg2 == m2, col, big), axis=-1, keepdims=True)
    e = jnp.exp(m2 - m1)
    w1 = 1.0 / (1.0 + e)
    w2 = e / (1.0 + e)
    idx = jnp.where(col == 0, i1, jnp.where(col == 1, i2, 0))
    gate = jnp.where(col == 0, w1, jnp.where(col == 1, w2, 0.0))
    return idx, gate


def _mm_ln_kernel(*refs, alpha, part_blocks, n_experts):
    n_parts = len(part_blocks)
    lhs_refs = refs[:n_parts]
    rhs_ref, x_hbm, g_ref, b_ref = refs[n_parts:n_parts + 4]
    if n_experts:
        wr_ref, of_ref, ob_ref, idx_ref, gate_ref, xbuf, xsem = refs[n_parts + 4:]
    else:
        of_ref, ob_ref, xbuf, xsem = refs[n_parts + 4:]
    kk = pl.program_id(1)
    tm = xbuf.shape[0]
    x_copy = pltpu.make_async_copy(x_hbm.at[pl.ds(pl.program_id(0) * tm, tm), :], xbuf, xsem)

    @pl.when(kk == 0)
    def _():
        x_copy.start()

    @pl.when(kk == 0)
    def _():
        of_ref[...] = jnp.dot(lhs_refs[0][...], rhs_ref[...], preferred_element_type=F32)

    first = 0
    for lhs_ref, blocks in zip(lhs_refs, part_blocks):
        @pl.when(jnp.logical_and(kk >= max(first, 1), kk < first + blocks))
        def _(lhs_ref=lhs_ref):
            of_ref[...] += jnp.dot(lhs_ref[...], rhs_ref[...], preferred_element_type=F32)
        first += blocks

    @pl.when(kk == pl.num_programs(1) - 1)
    def _():
        x_copy.wait()
        y = _layer_norm_rows(alpha * xbuf[...] + of_ref[...], g_ref[...], b_ref[...])
        of_ref[...] = y
        ob_ref[...] = y.astype(BF16)
        if n_experts:
            idx_ref[...], gate_ref[...] = _top2_gates(y, wr_ref[...], n_experts)


def _matmul_ln(lhs_parts, rhs, x, g, b, *, alpha, tm, tk, name, w_router=None):
    m = lhs_parts[0].shape[0]
    n = rhs.shape[1]
    tm = _pick(m, tm, V7X_SUBLANES * 2)
    tk = _pick(math.gcd(*[p.shape[1] for p in lhs_parts]), tk, V7X_LANES)
    part_blocks = tuple(p.shape[1] // tk for p in lhs_parts)

    def part_spec(first, blocks):
        return pl.BlockSpec((tm, tk),
                            lambda i, kk: (i, jnp.clip(kk - first, 0, blocks - 1)))

    def row_block(width):
        return pl.BlockSpec((tm, width), lambda i, kk: (i, 0))

    firsts = [sum(part_blocks[:p]) for p in range(len(part_blocks))]
    in_specs = [part_spec(f, nb) for f, nb in zip(firsts, part_blocks)] + [
        pl.BlockSpec((tk, n), lambda i, kk: (kk, 0)),
        pl.BlockSpec(memory_space=pl.ANY),
        pl.BlockSpec((1, n), lambda i, kk: (0, 0)),
        pl.BlockSpec((1, n), lambda i, kk: (0, 0)),
    ]
    operands = [*lhs_parts, rhs, x, g.reshape(1, n), b.reshape(1, n)]
    out_specs = [row_block(n), row_block(n)]
    out_shape = [jax.ShapeDtypeStruct((m, n), F32), jax.ShapeDtypeStruct((m, n), BF16)]
    n_experts = 0
    if w_router is not None:
        n_experts = w_router.shape[1]
        w_pad = jnp.zeros((n, V7X_LANES), F32).at[:, :n_experts].set(w_router.astype(F32))
        in_specs.append(pl.BlockSpec((n, V7X_LANES), lambda i, kk: (0, 0)))
        operands.append(w_pad)
        out_specs += [row_block(V7X_LANES), row_block(V7X_LANES)]
        out_shape += [jax.ShapeDtypeStruct((m, V7X_LANES), jnp.int32),
                      jax.ShapeDtypeStruct((m, V7X_LANES), F32)]
    outs = pl.pallas_call(
        functools.partial(_mm_ln_kernel, alpha=alpha, part_blocks=part_blocks,
                          n_experts=n_experts),
        grid=(m // tm, sum(part_blocks)),
        in_specs=in_specs,
        out_specs=out_specs,
        out_shape=out_shape,
        scratch_shapes=[pltpu.VMEM((tm, n), F32), pltpu.SemaphoreType.DMA(())],
        compiler_params=_params(("parallel", "arbitrary")),
        name=name,
    )(*operands)
    if w_router is None:
        return outs
    return outs[0], outs[1], outs[2][:, :TOP_K], outs[3][:, :TOP_K]


def _attn_kernel(lam_ref, w_ref, q_ref, k_ref, v_ref, o_ref, m_sc, l_sc, acc_sc, *,
                 tq, tk, tk_diag, d, lam_init):
    qi = pl.program_id(2)
    hw = 2 * d
    lanes = V7X_LANES
    m_sc[...] = jnp.full(m_sc.shape, MASK_VALUE, F32)
    l_sc[...] = jnp.zeros_like(l_sc)
    acc_sc[...] = jnp.zeros_like(acc_sc)

    def step(start, width, diag_block):
        r0 = 0 if diag_block is None else diag_block * width
        nr = tq - r0
        start = pl.multiple_of(start, width)
        kb = k_ref[pl.ds(start, width), :]
        vb = v_ref[pl.ds(start, width), :]
        if diag_block is not None:
            row = lax.broadcasted_iota(jnp.int32, (nr, width), 0) // ATTN_CHUNK
            col = lax.broadcasted_iota(jnp.int32, (nr, width), 1) // ATTN_CHUNK
            allowed = col <= row
        scores = [lax.dot_general(q_ref[r0:tq, mp * d:(mp + 1) * d], kb[:, mp * d:(mp + 1) * d],
                                  (((1,), (1,)), ((), ())), preferred_element_type=F32)
                  for mp in range(2)]
        for mp in range(2):
            s = scores[mp]
            if diag_block is not None:
                s = jnp.where(allowed, s, MASK_VALUE)
            m_prev = m_sc[mp, r0:tq]
            m_next = jnp.maximum(m_prev, jnp.max(s, axis=-1, keepdims=True))
            alpha = jnp.exp2(m_prev - m_next)
            ps = [jnp.exp2(s[:, c * lanes:(c + 1) * lanes] - m_next) for c in range(width // lanes)]
            psum = ps[0]
            for pc in ps[1:]:
                psum = psum + pc
            l_sc[mp, r0:tq] = alpha * l_sc[mp, r0:tq] + psum
            p = jnp.concatenate(ps, axis=1).astype(BF16)
            alpha_w = jnp.concatenate([alpha] * (hw // lanes), axis=1)
            acc_sc[mp, r0:tq] = alpha_w * acc_sc[mp, r0:tq] + jnp.dot(
                p, vb, preferred_element_type=F32)
            m_sc[mp, r0:tq] = m_next

    def body(j, carry):
        step(j * tk, tk, None)
        return carry

    lax.fori_loop(0, qi * (tq // tk), body, 0)
    for c in range(tq // tk_diag):
        step(qi * tq + c * tk_diag, tk_diag, c)

    lam_p = lam_ref[...]
    lam = (jnp.exp(jnp.sum(lam_p[0:1] * lam_p[1:2], axis=-1, keepdims=True))
           - jnp.exp(jnp.sum(lam_p[2:3] * lam_p[3:4], axis=-1, keepdims=True)) + lam_init)
    l0 = jnp.sum(l_sc[0], axis=-1, keepdims=True)
    l1 = jnp.sum(l_sc[1], axis=-1, keepdims=True)
    o = acc_sc[0] / l0 - lam * (acc_sc[1] / l1)
    o = o * lax.rsqrt(jnp.mean(o * o, axis=-1, keepdims=True) + RMS_EPS)
    o_ref[...] = (o * w_ref[...] * (1.0 - lam_init)).astype(o_ref.dtype)


def _diff_attention(h, lam_params, subln_w, *, n_heads, d, lam_init, tq, tk, tk_diag):
    bsz, seq, _ = h.shape
    hw = 2 * d
    tq = _pick(seq, tq, V7X_LANES)
    tk = _pick(tq, tk, V7X_LANES)
    tk_diag = _pick(tq, tk_diag, V7X_LANES)
    return pl.pallas_call(
        functools.partial(_attn_kernel, tq=tq, tk=tk, tk_diag=tk_diag, d=d, lam_init=lam_init),
        grid=(bsz, n_heads, seq // tq),
        in_specs=[
            pl.BlockSpec((4, d), lambda b, hd, i: (0, 0)),
            pl.BlockSpec((1, hw), lambda b, hd, i: (0, 0)),
            pl.BlockSpec((None, tq, hw), lambda b, hd, i: (b, i, hd)),
            pl.BlockSpec((None, seq, hw), lambda b, hd, i: (b, 0, n_heads + hd)),
            pl.BlockSpec((None, seq, hw), lambda b, hd, i: (b, 0, 2 * n_heads + hd)),
        ],
        out_specs=pl.BlockSpec((None, tq, hw), lambda b, hd, i: (b, i, hd)),
        out_shape=jax.ShapeDtypeStruct((bsz, seq, n_heads * hw), BF16),
        scratch_shapes=[pltpu.VMEM((2, tq, V7X_LANES), F32), pltpu.VMEM((2, tq, V7X_LANES), F32),
                        pltpu.VMEM((2, tq, hw), F32)],
        compiler_params=_params(("parallel", "parallel", "arbitrary")),
        name="diff_attention",
    )(lam_params, subln_w.reshape(1, hw), h, h, h)


def _s5_kernel(u_ref, acol_ref, arow_ref, bt_ref, bx_ref, ct_ref, d_ref, perm_ref, permt_ref,
               o_ref, m_ref, u_all, y_all, *, tc, p, n, n_chunks, rows):
    tcp = tc * p
    lanes = V7X_LANES
    spb = lanes // p
    gpb = lanes // p
    n_blk = tcp // lanes
    gi = pl.program_id(1)

    @pl.when(gi == 0)
    def _():
        for k in range(n_blk):
            zk = jnp.concatenate(
                [u_ref[pl.ds(k * spb + s, rows, stride=tc), :] for s in range(spb)], axis=1)
            ok = jnp.dot(zk.astype(BF16), perm_ref[...], preferred_element_type=F32)
            for g2 in range(gpb):
                u_all[g2, :, k * lanes:(k + 1) * lanes] = (
                    ok[:, g2 * lanes:(g2 + 1) * lanes].astype(BF16))

    u = u_all[gi]

    ar = acol_ref[:, 0:1]
    ai = acol_ref[:, 1:2]
    dt = acol_ref[:, 2:3]
    lane_id = lax.broadcasted_iota(jnp.int32, (1, lanes), 1)
    tau0 = (lane_id // p).astype(F32)

    def powers(t):
        mag = jnp.exp((ar * dt) * t)
        ang = (ai * dt) * t
        return mag * jnp.cos(ang), mag * jnp.sin(ang)

    def cmul2(a_re, a_im, b_re, b_im):
        return a_re * b_re - a_im * b_im, a_re * b_im + a_im * b_re

    mag1 = jnp.exp(ar * dt)
    lr = mag1 * jnp.cos(ai * dt)
    li = mag1 * jnp.sin(ai * dt)
    den = ar * ar + ai * ai
    f_re = ((lr - 1.0) * ar + li * ai) / den
    f_im = (li * ar - (lr - 1.0) * ai) / den
    bb_re, bb_im = cmul2(f_re, f_im, bt_ref[0], bt_ref[1])

    blk_re, blk_im = powers((lane_id * spb).astype(F32))
    zb_re, zb_im = cmul2(*powers((spb - 1.0) - tau0), bb_re, bb_im)
    vb_re, vb_im = cmul2(*powers(tau0), ct_ref[0], ct_ref[1])
    z_blocks, v_blocks = [], []
    for k in range(n_blk):
        kr = n_blk - 1 - k
        z_blocks.append(cmul2(zb_re, zb_im, blk_re[:, kr:kr + 1], blk_im[:, kr:kr + 1]))
        v_blocks.append(cmul2(vb_re, vb_im, blk_re[:, k:k + 1], blk_im[:, k:k + 1]))
    z_re = jnp.concatenate([z[0] for z in z_blocks], axis=1)
    z_im = jnp.concatenate([z[1] for z in z_blocks], axis=1)
    wt = jnp.concatenate([z_re, z_im], axis=0).astype(BF16)
    v_re = jnp.concatenate([v[0] for v in v_blocks], axis=1)
    v_im = jnp.concatenate([v[1] for v in v_blocks], axis=1)
    vmat = jnp.concatenate([v_re, -v_im], axis=0)

    ar2 = arow_ref[0:1, :]
    ai2 = arow_ref[1:2, :]
    dt2 = arow_ref[2:3, :]
    half = lax.broadcasted_iota(jnp.int32, (1, 2 * n), 1) < n
    sgn = jnp.where(half, -1.0, 1.0).astype(F32)
    mag1r = jnp.exp(ar2 * dt2)
    lr2 = mag1r * jnp.cos(ai2 * dt2)
    li2 = mag1r * jnp.sin(ai2 * dt2)
    den2 = ar2 * ar2 + ai2 * ai2
    fr2 = ((lr2 - 1.0) * ar2 + li2 * ai2) / den2
    fi2 = (li2 * ar2 - (lr2 - 1.0) * ai2) / den2
    bbt = fr2 * bx_ref[0] + (sgn * fi2) * bx_ref[1]
    kq = jnp.dot(bbt, vmat, preferred_element_type=F32, precision=lax.Precision.HIGHEST)

    lane = lax.broadcasted_iota(jnp.int32, (p, tcp), 1)
    m_ref[0:p, :] = kq.astype(BF16)
    for s in range(1, tc):
        blk = jnp.where(lane >= s * p, pltpu.roll(kq, s * p, axis=1), 0.0)
        m_ref[s * p:(s + 1) * p, :] = blk.astype(BF16)

    def cmul(x, r2, i2s):
        return x * r2 + pltpu.roll(x, n, axis=1) * i2s

    st = lax.dot_general(u, wt, (((1,), (1,)), ((), ())), preferred_element_type=F32)
    cidx = lax.broadcasted_iota(jnp.int32, (rows, 2 * n), 0) % n_chunks
    mag_c = jnp.exp(ar2 * dt2 * tc)
    ang_c = ai2 * dt2 * tc
    r2 = mag_c * jnp.cos(ang_c)
    i2 = sgn * mag_c * jnp.sin(ang_c)
    k = 1
    while k < n_chunks:
        sh = jnp.where(cidx >= k, pltpu.roll(st, k, axis=0), 0.0)
        st = st + cmul(sh, r2, i2)
        r2, i2 = r2 * r2 - i2 * i2, 2.0 * r2 * i2
        k *= 2
    hin = jnp.where(cidx >= 1, pltpu.roll(st, 1, axis=0), 0.0)
    hp = cmul(hin, lr2, sgn * li2)

    y = jnp.dot(u, m_ref[...], preferred_element_type=F32)
    y = y + jnp.dot(hp.astype(BF16), vmat.astype(BF16), preferred_element_type=F32)
    y = y + u.astype(F32) * d_ref[...]
    g = 0.5 * y * (1.0 + lax.erf(y * (2.0 ** -0.5)))
    y_all[gi] = g.astype(BF16)

    @pl.when(gi == gpb - 1)
    def _():
        for k in range(n_blk):
            wk = jnp.concatenate(
                [y_all[g2, :, k * lanes:(k + 1) * lanes] for g2 in range(gpb)], axis=1)
            ok = jnp.dot(wk, permt_ref[...], preferred_element_type=F32)
            for s in range(spb):
                o_ref[pl.ds(k * spb + s, rows, stride=tc), :] = ok[:, s * lanes:(s + 1) * lanes]


def _s5_gelu(u, a_re, a_im, log_dt, b_re, b_im, c_re, c_im, d_skip, *, bsz):
    t, width = u.shape
    seq = t // bsz
    n_groups, n = a_re.shape
    p = b_re.shape[-1]
    tc = S5_CHUNK
    n_chunks = seq // tc
    rows = bsz * n_chunks
    tcp = tc * p
    lanes = V7X_LANES
    gpb = lanes // p
    dt = jnp.exp(log_dt.astype(F32))
    dtb = jnp.broadcast_to(dt[:, None], (n_groups, n))
    acol = jnp.stack([a_re.astype(F32), a_im.astype(F32), dtb], axis=-1)
    arow = jnp.stack([jnp.tile(a_re.astype(F32), (1, 2)), jnp.tile(a_im.astype(F32), (1, 2)),
                      jnp.tile(dtb, (1, 2))], axis=1)
    spb = lanes // p
    bt = jnp.stack([jnp.tile(b_re.astype(F32), (1, 1, spb)),
                    jnp.tile(b_im.astype(F32), (1, 1, spb))], axis=1)
    btr = jnp.swapaxes(b_re.astype(F32), 1, 2)
    bti = jnp.swapaxes(b_im.astype(F32), 1, 2)
    bx = jnp.stack([jnp.concatenate([btr, bti], axis=-1),
                    jnp.concatenate([bti, btr], axis=-1)], axis=1)
    ctr = jnp.swapaxes(c_re.astype(F32), 1, 2)
    cti = jnp.swapaxes(c_im.astype(F32), 1, 2)
    ct = jnp.stack([jnp.tile(ctr, (1, 1, spb)), jnp.tile(cti, (1, 1, spb))], axis=1)
    dtile = jnp.tile(d_skip.astype(F32).reshape(n_groups, 1, p), (1, 1, tc))

    src = jnp.arange(spb * lanes, dtype=jnp.int32)
    dst = ((src % lanes) // p) * (spb * p) + (src // lanes) * p + src % p
    perm = (dst[:, None] == src[None, :]).astype(BF16)
    permt = perm.T

    def grp(*tail):
        return lambda cb, gi: (cb * gpb + gi,) + tail

    return pl.pallas_call(
        functools.partial(_s5_kernel, tc=tc, p=p, n=n, n_chunks=n_chunks, rows=rows),
        grid=(n_groups // gpb, gpb),
        in_specs=[
            pl.BlockSpec((t, lanes), lambda cb, gi: (0, cb)),
            pl.BlockSpec((None, n, 3), grp(0, 0)),
            pl.BlockSpec((None, 3, 2 * n), grp(0, 0)),
            pl.BlockSpec((None, 2, n, lanes), grp(0, 0, 0)),
            pl.BlockSpec((None, 2, p, 2 * n), grp(0, 0, 0)),
            pl.BlockSpec((None, 2, n, lanes), grp(0, 0, 0)),
            pl.BlockSpec((None, 1, tcp), grp(0, 0)),
            pl.BlockSpec((spb * lanes, spb * lanes), lambda cb, gi: (0, 0)),
            pl.BlockSpec((spb * lanes, spb * lanes), lambda cb, gi: (0, 0)),
        ],
        out_specs=pl.BlockSpec((t, lanes), lambda cb, gi: (0, cb)),
        out_shape=jax.ShapeDtypeStruct((t, width), F32),
        scratch_shapes=[pltpu.VMEM((tcp, tcp), BF16), pltpu.VMEM((gpb, rows, tcp), BF16),
                        pltpu.VMEM((gpb, rows, tcp), BF16)],
        compiler_params=_params(("parallel", "arbitrary")),
        name="s5_gelu",
    )(u, acol, arow, bt, bx, ct, dtile, perm, permt)


GATHER_UNROLL = 8


def _moe_gate_up_kernel(te_ref, tv_ref, tok_ref, x_hbm, wg_ref, wu_ref, o_ref, xbuf, xb, sem, *,
                        tm):
    i = pl.program_id(0)
    j = pl.program_id(1)
    n_tiles = pl.num_programs(0)
    slot = i % 2
    valid = tv_ref[i] != 0

    def issue_tile(tile, dst_slot):
        def issue(grp, carry):
            for lane in range(GATHER_UNROLL):
                r = grp * GATHER_UNROLL + lane
                pltpu.make_async_copy(x_hbm.at[pl.ds(tok_ref[tile * tm + r], 1), :],
                                      xbuf.at[dst_slot, pl.ds(r, 1), :],
                                      sem.at[dst_slot]).start(priority=lane % 2)
            return carry
        lax.fori_loop(0, tm // GATHER_UNROLL, issue, 0)

    def wait_tile(dst_slot):
        pltpu.make_async_copy(x_hbm.at[pl.ds(0, tm), :], xbuf.at[dst_slot],
                              sem.at[dst_slot]).wait()

    @pl.when(jnp.logical_and(j == 0, i == 0))
    def _():
        issue_tile(0, 0)

    @pl.when(jnp.logical_and(j == 0, valid))
    def _():
        wait_tile(slot)

    nxt = jnp.minimum(i + 1, n_tiles - 1)

    @pl.when(jnp.logical_and(j == 0, jnp.logical_and(i + 1 < n_tiles, tv_ref[nxt] != 0)))
    def _():
        issue_tile(i + 1, 1 - slot)

    @pl.when(jnp.logical_and(j == 0, valid))
    def _():
        xb[...] = xbuf[slot].astype(BF16)

    @pl.when(valid)
    def _():
        a = xb[...]
        gate = jnp.dot(a, wg_ref[...], preferred_element_type=F32)
        up = jnp.dot(a, wu_ref[...], preferred_element_type=F32)
        o_ref[...] = _epi_swiglu([gate, up], [], []).astype(o_ref.dtype)

    @pl.when(jnp.logical_not(valid))
    def _():
        o_ref[...] = jnp.zeros_like(o_ref)


def _moe_gate_up(x, w_gate, w_up, row_token, tile_expert, tile_valid, *, tm, tn):
    k = x.shape[1]
    n_rows = row_token.shape[0]
    n = w_gate.shape[2]
    tn = _pick(n, tn, V7X_LANES)
    return pl.pallas_call(
        functools.partial(_moe_gate_up_kernel, tm=tm),
        grid_spec=pltpu.PrefetchScalarGridSpec(
            num_scalar_prefetch=3,
            grid=(n_rows // tm, n // tn),
            in_specs=[pl.BlockSpec(memory_space=pl.ANY),
                      pl.BlockSpec((None, k, tn), lambda i, j, te, tv, tok: (te[i], 0, j)),
                      pl.BlockSpec((None, k, tn), lambda i, j, te, tv, tok: (te[i], 0, j))],
            out_specs=pl.BlockSpec((tm, tn), lambda i, j, te, tv, tok: (i, j)),
            scratch_shapes=[pltpu.VMEM((2, tm, k), x.dtype), pltpu.VMEM((tm, k), BF16),
                            pltpu.SemaphoreType.DMA((2,))],
        ),
        out_shape=jax.ShapeDtypeStruct((n_rows, n), BF16),
        compiler_params=_params(("arbitrary", "arbitrary")),
        name="moe_gate_up",
    )(tile_expert, tile_valid, row_token, x, w_gate, w_up)


def _grouped_kernel(te_ref, tv_ref, *refs, n_rhs, n_row, epilogue):
    lhs_ref = refs[0]
    rhs_refs = refs[1:1 + n_rhs]
    row_refs = refs[1 + n_rhs:1 + n_rhs + n_row]
    o_ref = refs[-1]

    valid = tv_ref[pl.program_id(0)] != 0

    @pl.when(valid)
    def _():
        a = lhs_ref[...]
        accs = [jnp.dot(a, r[...], preferred_element_type=F32) for r in rhs_refs]
        o_ref[...] = epilogue(accs, [r[...] for r in row_refs], []).astype(o_ref.dtype)

    @pl.when(jnp.logical_not(valid))
    def _():
        o_ref[...] = jnp.zeros_like(o_ref)


def _grouped_matmul(lhs, rhss, tile_expert, tile_valid, epilogue, out_dtype, *, tm, tn,
                    row_extras=(), name):
    r, k = lhs.shape
    n = rhss[0].shape[2]
    tn = _pick(n, tn, V7X_LANES)
    in_specs = [pl.BlockSpec((tm, k), lambda i, j, te, tv: (i, 0))]
    in_specs += [pl.BlockSpec((None, k, tn), lambda i, j, te, tv: (te[i], 0, j)) for _ in rhss]
    in_specs += [pl.BlockSpec((tm, 1), lambda i, j, te, tv: (i, 0)) for _ in row_extras]
    return pl.pallas_call(
        functools.partial(_grouped_kernel, n_rhs=len(rhss), n_row=len(row_extras),
                          epilogue=epilogue),
        grid_spec=pltpu.PrefetchScalarGridSpec(
            num_scalar_prefetch=2,
            grid=(r // tm, n // tn),
            in_specs=in_specs,
            out_specs=pl.BlockSpec((tm, tn), lambda i, j, te, tv: (i, j)),
        ),
        out_shape=jax.ShapeDtypeStruct((r, n), out_dtype),
        compiler_params=_params(("parallel", "arbitrary")),
        name=name,
    )(tile_expert, tile_valid, lhs, *rhss, *row_extras)


def _epi_row_scale(accs, rows, cols):
    return accs[0] * rows[0]


def _combine_ln_kernel(pos_ref, y_hbm, x_ref, g_ref, b_ref, of_ref, ob_ref, buf, sem, *,
                       tc, n_tokens, alpha):
    i = pl.program_id(0)
    slot = i % 2

    def issue_tile(tile, dst_slot):
        base = tile * tc

        def issue(r, carry):
            for kk in range(TOP_K):
                pltpu.make_async_copy(y_hbm.at[pl.ds(pos_ref[kk * n_tokens + base + r], 1), :],
                                      buf.at[dst_slot, kk, pl.ds(r, 1), :],
                                      sem.at[dst_slot]).start(priority=kk % 2)
            return carry
        lax.fori_loop(0, tc, issue, 0, unroll=GATHER_UNROLL)

    @pl.when(i == 0)
    def _():
        issue_tile(0, 0)

    for kk in range(TOP_K):
        pltpu.make_async_copy(y_hbm.at[pl.ds(0, tc), :], buf.at[slot, kk], sem.at[slot]).wait()

    @pl.when(i + 1 < pl.num_programs(0))
    def _():
        issue_tile(i + 1, 1 - slot)

    f = buf[slot, 0]
    for kk in range(1, TOP_K):
        f = f + buf[slot, kk]
    y = _layer_norm_rows(alpha * x_ref[...] + f, g_ref[...], b_ref[...])
    of_ref[...] = y
    ob_ref[...] = y.astype(BF16)


def _combine_ln(y_sorted, pos, x, g, b, *, alpha, tc):
    t, dm = x.shape
    tc = _pick(t, tc, V7X_SUBLANES * 2)
    return pl.pallas_call(
        functools.partial(_combine_ln_kernel, tc=tc, n_tokens=t, alpha=alpha),
        grid_spec=pltpu.PrefetchScalarGridSpec(
            num_scalar_prefetch=1,
            grid=(t // tc,),
            in_specs=[pl.BlockSpec(memory_space=pl.ANY),
                      pl.BlockSpec((tc, dm), lambda i, pos: (i, 0)),
                      pl.BlockSpec((1, dm), lambda i, pos: (0, 0)),
                      pl.BlockSpec((1, dm), lambda i, pos: (0, 0))],
            out_specs=[pl.BlockSpec((tc, dm), lambda i, pos: (i, 0)),
                       pl.BlockSpec((tc, dm), lambda i, pos: (i, 0))],
            scratch_shapes=[pltpu.VMEM((2, TOP_K, tc, dm), y_sorted.dtype),
                            pltpu.SemaphoreType.DMA((2,))],
        ),
        out_shape=[jax.ShapeDtypeStruct((t, dm), F32), jax.ShapeDtypeStruct((t, dm), BF16)],
        compiler_params=_params(("arbitrary",)),
        name="moe_combine_ln",
    )(pos, y_sorted, x, g.reshape(1, dm), b.reshape(1, dm))


def _routing_tables(top_idx, top_w, n_experts, tm):
    t = top_idx.shape[0]
    n_pairs = TOP_K * t
    n_rows = n_pairs + n_experts * tm
    n_tiles = n_rows // tm
    e_flat = top_idx.T.reshape(n_pairs)
    w_flat = top_w.T.reshape(n_pairs)
    onehot = (e_flat[:, None] == jnp.arange(n_experts, dtype=jnp.int32)[None, :]).astype(jnp.int32)
    rank = jnp.sum((jnp.cumsum(onehot, axis=0) - onehot) * onehot, axis=1)
    counts = jnp.sum(onehot, axis=0)
    tiles_per = (counts + tm - 1) // tm
    tile_end = jnp.cumsum(tiles_per)
    row_start = (tile_end - tiles_per) * tm
    pos = (row_start[e_flat] + rank).astype(jnp.int32)
    row_pair = jnp.full((n_rows,), -1, jnp.int32).at[pos].set(
        jnp.arange(n_pairs, dtype=jnp.int32))
    has_pair = row_pair >= 0
    row_token = jnp.where(has_pair, row_pair % t, 0)
    row_gate = jnp.where(has_pair, w_flat[jnp.maximum(row_pair, 0)], 0.0)
    tile_ids = jnp.arange(n_tiles, dtype=jnp.int32)
    total = tile_end[-1]
    tile_valid = (tile_ids < total).astype(jnp.int32)
    clipped = jnp.minimum(tile_ids, total - 1)
    tile_expert = jnp.minimum(
        jnp.sum((tile_end[None, :] <= clipped[:, None]).astype(jnp.int32), axis=1),
        n_experts - 1).astype(jnp.int32)
    return pos, row_token, row_gate.reshape(n_rows, 1), tile_expert, tile_valid


def kernel(x, w_in, w_out, attn_lambda_q1, attn_lambda_k1, attn_lambda_q2, attn_lambda_k2, attn_subln_w, ssm_a_re, ssm_a_im, ssm_log_dt, ssm_b_re, ssm_b_im, ssm_c_re, ssm_c_im, ssm_d, ssm_w_glu, ln1_g, ln1_b, ln2_g, ln2_b, ffn_w_gate, ffn_w_up, ffn_w_down, moe_w_router, moe_w_gate, moe_w_up, moe_w_down):
    bsz, seq, dm = x.shape
    depth = w_in.shape[0]
    d = attn_lambda_q1.shape[-1]
    ssm_width = ssm_d.shape[-1]
    attn_width = w_out.shape[1] - ssm_width
    n_heads = attn_width // (2 * d)
    n_experts = moe_w_router.shape[-1]
    t = bsz * seq
    alpha = (2.0 * depth) ** 0.25
    moe_tm = _pick(t, 512, V7X_SUBLANES * 2)

    xf = x.reshape(t, dm).astype(F32)
    xb = xf
    col_scale = jnp.concatenate([jnp.full((1, attn_width), d ** -0.5 * math.log2(math.e), F32),
                                 jnp.ones((1, 2 * attn_width), F32)], axis=1)

    for l in range(depth):
        lam_init = 0.8 - 0.6 * math.exp(-0.3 * l)
        w_in_l = w_in[l].astype(BF16)
        qkv_width = 3 * attn_width
        h = _matmul(xb, [w_in_l], _epi_scale_cols, BF16, tm=1024, tn=512,
                    col_extras=[col_scale], n_out=qkv_width, name=f"w_in_qkv_{l}")
        h = h.reshape(bsz, seq, qkv_width)
        u = _matmul(xb, [w_in_l], _epi_first, F32, tm=1024, tn=512, rhs_first_col=qkv_width,
                    name=f"w_in_u_{l}")
        lam_params = jnp.stack([attn_lambda_q1[l], attn_lambda_k1[l],
                                attn_lambda_q2[l], attn_lambda_k2[l]]).astype(F32)
        o = _diff_attention(h, lam_params, attn_subln_w[l].astype(F32), n_heads=n_heads, d=d,
                            lam_init=lam_init, tq=1024, tk=1024, tk_diag=512)
        g = _s5_gelu(u, ssm_a_re[l], ssm_a_im[l], ssm_log_dt[l], ssm_b_re[l], ssm_b_im[l],
                     ssm_c_re[l], ssm_c_im[l], ssm_d[l], bsz=bsz)
        y = _matmul(g, [ssm_w_glu[l].astype(BF16)], _epi_glu, BF16, tm=1024, tn=512,
                    row_extras=[g], name=f"ssm_glu_{l}")
        is_moe = l % 2 == 1
        mixed = _matmul_ln([o.reshape(t, attn_width), y], w_out[l].astype(BF16), xf,
                           ln1_g[l].astype(F32), ln1_b[l].astype(F32), alpha=alpha,
                           tm=512, tk=512, name=f"w_out_ln_{l}",
                           w_router=moe_w_router[l // 2] if is_moe else None)
        xf, xb = mixed[0], mixed[1]
        if not is_moe:
            e = l // 2
            d_ff = ffn_w_gate.shape[-1]
            pad = (-d_ff) % FFN_K_TILE
            w_down = jnp.pad(ffn_w_down[e].astype(BF16), ((0, pad), (0, 0)))
            hid = _matmul(xb, [ffn_w_gate[e].astype(BF16), ffn_w_up[e].astype(BF16)],
                          _epi_swiglu, BF16, tm=1024, tn=256, n_out=d_ff + pad,
                          name=f"ffn_gate_up_{l}")
            xf, xb = _matmul_ln([hid], w_down, xf, ln2_g[l].astype(F32),
                                ln2_b[l].astype(F32), alpha=alpha, tm=512, tk=FFN_K_TILE,
                                name=f"ffn_down_ln_{l}")
        else:
            e = l // 2
            top_idx, top_w = mixed[2], mixed[3]
            pos, row_token, row_gate, tile_expert, tile_valid = _routing_tables(
                top_idx, top_w, n_experts, moe_tm)
            hid = _moe_gate_up(xf, moe_w_gate[e].astype(BF16), moe_w_up[e].astype(BF16),
                               row_token, tile_expert, tile_valid, tm=moe_tm, tn=512)
            ys = _grouped_matmul(hid, [moe_w_down[e].astype(BF16)], tile_expert, tile_valid,
                                 _epi_row_scale, F32, tm=moe_tm, tn=1024,
                                 row_extras=[row_gate], name=f"moe_down_{l}")
            xf, xb = _combine_ln(ys, pos, xf, ln2_g[l].astype(F32), ln2_b[l].astype(F32),
                                 alpha=alpha, tc=256)
    return xf.reshape(bsz, seq, dm).astype(x.dtype)
```
